```python
import jax, jax.numpy as jnp
from jax import lax
import numpy as np

D_MODEL = 1024
BATCH = 4
SEQ = 4096
DEPTH = 4
DEC_BATCH = 128
DEC_SEQ = 4
PAST_LEN = 2048
PAGE_SIZE = 128

N_MIXERS = 3
N_HEADS = 16
HEAD_DIM = D_MODEL // N_HEADS
Q_BLOCK = 128
SB_SPAN_MIN = 4.0
SB_SPAN_MAX = 16384.0
CONV_W = 3
POOL_WINDOWS = (2, 4, 8, 16)
N_POOL_GROUPS = len(POOL_WINDOWS)
POOL_GROUP = D_MODEL // N_POOL_GROUPS
POOL_HIST = max(POOL_WINDOWS) - 1
D_FF = ((8 * D_MODEL // 3 + 127) // 128) * 128
FFN_HALF = 0.5
RMS_EPS = 1e-6
N_SUB = 3
N_MOD = 3
N_ATTN_LAYERS = len(range(0, DEPTH, N_MIXERS))
N_CONV_LAYERS = len(range(1, DEPTH, N_MIXERS))
N_POOL_LAYERS = len(range(2, DEPTH, N_MIXERS))

kernel_name = "hybrid_stickbreak_conv_pool_decoder_step"


def rms_norm(x, g):
    xf = x.astype(jnp.float32)
    y = xf * lax.rsqrt(jnp.mean(xf * xf, axis=-1, keepdims=True) + RMS_EPS)
    return (y * g.astype(jnp.float32)).astype(x.dtype)


def swiglu(h, wg, wu, wd):
    return (jax.nn.silu(h @ wg) * (h @ wu)) @ wd


def stick_breaking_block(q, k, v, q_pos, k_pos, bias):
    z = jnp.einsum("bqhd,bkhd->bhqk", q, k, preferred_element_type=jnp.float32) * (HEAD_DIM ** -0.5)
    z = z + bias.astype(jnp.float32)[None, :, None, None]
    mask = k_pos[None, :] < q_pos[:, None]
    log_stay = jnp.where(mask, jax.nn.log_sigmoid(-z), 0.0)
    log_stick = lax.cumsum(log_stay, axis=3, reverse=True) - log_stay
    a = jnp.where(mask, jnp.exp(jax.nn.log_sigmoid(z) + log_stick), 0.0)
    return jnp.einsum("bhqk,bkhd->bqhd", a.astype(v.dtype), v)


def stick_breaking_attention(q, k, v, q_pos, k_pos, bias):
    b, t, h, d = q.shape
    if t <= Q_BLOCK:
        return stick_breaking_block(q, k, v, q_pos, k_pos, bias)
    nb = t // Q_BLOCK
    qb = q.reshape(b, nb, Q_BLOCK, h, d).swapaxes(0, 1)
    pb = q_pos.reshape(nb, Q_BLOCK)
    ob = lax.map(lambda qp: stick_breaking_block(qp[0], k, v, qp[1], k_pos, bias), (qb, pb))
    return ob.swapaxes(0, 1).reshape(b, t, h, d)


def sb_mixer(h, k_past, v_past, pos, w_qkv, w_o, sb_bias):
    b, t, _ = h.shape
    qkv = (h @ w_qkv).reshape(b, t, 3, N_HEADS, HEAD_DIM)
    q, k, v = qkv[:, :, 0], qkv[:, :, 1], qkv[:, :, 2]
    k_all = jnp.concatenate([k_past.astype(k.dtype), k], axis=1)
    v_all = jnp.concatenate([v_past.astype(v.dtype), v], axis=1)
    k_pos = jnp.arange(k_all.shape[1], dtype=jnp.int32)
    o = stick_breaking_attention(q, k_all, v_all, pos, k_pos, sb_bias)
    return o.reshape(b, t, D_MODEL) @ w_o, k, v


def short_conv_mixer(h, prefix, w_bcx, conv_w, w_out):
    t = h.shape[1]
    b_gate, c_gate, xin = jnp.split(h @ w_bcx, 3, axis=-1)
    u = c_gate * xin
    ext = jnp.concatenate([prefix.astype(u.dtype), u], axis=1)
    y = sum(conv_w[j] * ext[:, j:j + t] for j in range(CONV_W))
    return (b_gate * y) @ w_out, ext[:, -(CONV_W - 1):]


def pool_mixer(h, prefix, pos, w_pool, pool_scale):
    b, t, _ = h.shape
    ext = jnp.concatenate([prefix.astype(h.dtype), h], axis=1).astype(jnp.float32)
    cs = jnp.concatenate([jnp.zeros((b, 1, D_MODEL), jnp.float32), jnp.cumsum(ext, axis=1)], axis=1)
    end = cs[:, POOL_HIST + 1:POOL_HIST + 1 + t]
    means = []
    for g, w in enumerate(POOL_WINDOWS):
        sl = slice(g * POOL_GROUP, (g + 1) * POOL_GROUP)
        start = cs[:, POOL_HIST + 1 - w:POOL_HIST + 1 - w + t, sl]
        count = jnp.minimum(pos + 1, w).astype(jnp.float32)[None, :, None]
        means.append((end[:, :, sl] - start) / count)
    diff = jnp.concatenate(means, axis=-1).astype(h.dtype) - h
    mixed = jnp.einsum("btgc,gcd->btgd", diff.reshape(b, t, N_POOL_GROUPS, POOL_GROUP), w_pool)
    return mixed.reshape(b, t, D_MODEL) * pool_scale, ext[:, -POOL_HIST:].astype(h.dtype)


def trunk(x, c, pos, past_kv, conv_prefix, pool_prefix, w_ada, b_ada, g_pre, g_post,
          w_ffn_gate, w_ffn_up, w_ffn_down, w_qkv, w_o, sb_bias, w_bcx, conv_w, w_conv_out,
          w_pool, pool_scale):
    bsz = x.shape[0]
    new_k, new_v, new_conv, new_pool = [], [], [], []
    c_act = jax.nn.silu(c)
    for i in range(DEPTH):
        mod = (c_act @ w_ada[i] + b_ada[i]).reshape(bsz, N_SUB, N_MOD, 1, D_MODEL)

        def pre(z, j):
            return rms_norm(z, g_pre[i, j]) * (1 + mod[:, j, 1]) + mod[:, j, 0]

        def post(o, j, weight):
            return weight * mod[:, j, 2] * rms_norm(o, g_post[i, j])

        x = x + post(swiglu(pre(x, 0), w_ffn_gate[i, 0], w_ffn_up[i, 0], w_ffn_down[i, 0]), 0, FFN_HALF)
        h = pre(x, 1)
        kind, slot = i % N_MIXERS, i // N_MIXERS
        if kind == 0:
            k_past, v_past = past_kv(slot)
            o, k_new, v_new = sb_mixer(h, k_past, v_past, pos, w_qkv[slot], w_o[slot], sb_bias[slot])
            new_k.append(k_new)
            new_v.append(v_new)
        elif kind == 1:
            o, s_new = short_conv_mixer(h, conv_prefix[slot], w_bcx[slot], conv_w[slot], w_conv_out[slot])
            new_conv.append(s_new)
        else:
            o, s_new = pool_mixer(h, pool_prefix[slot], pos, w_pool[slot], pool_scale[slot])
            new_pool.append(s_new)
        x = x + post(o, 1, 1.0)
        x = x + post(swiglu(pre(x, 2), w_ffn_gate[i, 1], w_ffn_up[i, 1], w_ffn_down[i, 1]), 2, FFN_HALF)
    return x, jnp.stack(new_k), jnp.stack(new_v), jnp.stack(new_conv), jnp.stack(new_pool)


def setup_inputs(seed: int = 0) -> dict:
    key = jax.random.key(seed)
    ks = jax.random.split(key, 24)
    n_pages = PAST_LEN // PAGE_SIZE
    n_phys = (5 * DEC_BATCH * n_pages) // 4
    f32 = jnp.float32

    def nrm(k, shape, scale=1.0):
        return jax.random.normal(k, shape, f32) * scale

    page_table = jax.random.permutation(ks[0], n_phys)[:DEC_BATCH * n_pages].reshape(DEC_BATCH, n_pages).astype(jnp.int32)
    spans = jnp.exp(jnp.linspace(jnp.log(SB_SPAN_MIN), jnp.log(SB_SPAN_MAX), N_HEADS)).astype(f32)
    sb_bias = -jnp.log(spans)[None, :] + nrm(ks[23], (N_ATTN_LAYERS, N_HEADS), 0.1)
    return {
        "x_prompt": nrm(ks[1], (BATCH, SEQ, D_MODEL)),
        "x_sample": nrm(ks[2], (DEC_BATCH, DEC_SEQ, D_MODEL)),
        "cache_k": nrm(ks[3], (N_ATTN_LAYERS, n_phys, PAGE_SIZE, N_HEADS, HEAD_DIM)),
        "cache_v": nrm(ks[4], (N_ATTN_LAYERS, n_phys, PAGE_SIZE, N_HEADS, HEAD_DIM)),
        "state_conv": nrm(ks[5], (N_CONV_LAYERS, DEC_BATCH, CONV_W - 1, D_MODEL)),
        "state_pool": nrm(ks[6], (N_POOL_LAYERS, DEC_BATCH, POOL_HIST, D_MODEL)),
        "page_table": page_table,
        "c_prompt": nrm(ks[7], (BATCH, D_MODEL)),
        "c_sample": nrm(ks[8], (DEC_BATCH, D_MODEL)),
        "w_ada": nrm(ks[9], (DEPTH, D_MODEL, N_SUB * N_MOD * D_MODEL), 0.5 * D_MODEL ** -0.5),
        "b_ada": nrm(ks[10], (DEPTH, N_SUB * N_MOD * D_MODEL), 0.02),
        "g_pre": 1.0 + nrm(ks[11], (DEPTH, N_SUB, D_MODEL), 0.05),
        "g_post": 1.0 + nrm(ks[12], (DEPTH, N_SUB, D_MODEL), 0.05),
        "w_ffn_gate": nrm(ks[13], (DEPTH, 2, D_MODEL, D_FF), D_MODEL ** -0.5),
        "w_ffn_up": nrm(ks[14], (DEPTH, 2, D_MODEL, D_FF), D_MODEL ** -0.5),
        "w_ffn_down": nrm(ks[15], (DEPTH, 2, D_FF, D_MODEL), D_FF ** -0.5),
        "w_qkv": nrm(ks[16], (N_ATTN_LAYERS, D_MODEL, 3 * D_MODEL), D_MODEL ** -0.5),
        "w_o": nrm(ks[17], (N_ATTN_LAYERS, D_MODEL, D_MODEL), D_MODEL ** -0.5),
        "sb_bias": sb_bias,
        "w_bcx": nrm(ks[18], (N_CONV_LAYERS, D_MODEL, 3 * D_MODEL), D_MODEL ** -0.5),
        "conv_w": nrm(ks[19], (N_CONV_LAYERS, CONV_W, D_MODEL), CONV_W ** -0.5),
        "w_conv_out": nrm(ks[20], (N_CONV_LAYERS, D_MODEL, D_MODEL), D_MODEL ** -0.5),
        "w_pool": nrm(ks[21], (N_POOL_LAYERS, N_POOL_GROUPS, POOL_GROUP, POOL_GROUP), POOL_GROUP ** -0.5),
        "pool_scale": 1.0 + nrm(ks[22], (N_POOL_LAYERS, D_MODEL), 0.1),
    }


def reference(x_prompt, x_sample, cache_k, cache_v, state_conv, state_pool, page_table,
              c_prompt, c_sample, w_ada, b_ada, g_pre, g_post, w_ffn_gate, w_ffn_up,
              w_ffn_down, w_qkv, w_o, sb_bias, w_bcx, conv_w, w_conv_out, w_pool, pool_scale):
    weights = (w_ada, b_ada, g_pre, g_post, w_ffn_gate, w_ffn_up, w_ffn_down,
               w_qkv, w_o, sb_bias, w_bcx, conv_w, w_conv_out, w_pool, pool_scale)

    bp, tp = x_prompt.shape[0], x_prompt.shape[1]
    empty_kv = jnp.zeros((bp, 0, N_HEADS, HEAD_DIM), x_prompt.dtype)

    def prompt_past(slot):
        return empty_kv, empty_kv

    pos_p = jnp.arange(tp, dtype=jnp.int32)
    conv_zero = jnp.zeros((N_CONV_LAYERS, bp, CONV_W - 1, D_MODEL), x_prompt.dtype)
    pool_zero = jnp.zeros((N_POOL_LAYERS, bp, POOL_HIST, D_MODEL), x_prompt.dtype)
    y_prompt, k_prompt, v_prompt, conv_prompt, pool_prompt = trunk(
        x_prompt, c_prompt, pos_p, prompt_past, conv_zero, pool_zero, *weights)

    db, n_pages = page_table.shape
    past_len = n_pages * PAGE_SIZE

    def sample_past(slot):
        k = cache_k[slot][page_table].reshape(db, past_len, N_HEADS, HEAD_DIM)
        v = cache_v[slot][page_table].reshape(db, past_len, N_HEADS, HEAD_DIM)
        return k, v

    pos_s = past_len + jnp.arange(x_sample.shape[1], dtype=jnp.int32)
    y_sample, k_sample, v_sample, conv_sample, pool_sample = trunk(
        x_sample, c_sample, pos_s, sample_past, state_conv, state_pool, *weights)

    return (y_prompt, y_sample, k_prompt, v_prompt, conv_prompt, pool_prompt,
            k_sample, v_sample, conv_sample, pool_sample)
```

```python
import functools

import jax
import jax.numpy as jnp
from jax import lax
from jax.experimental import pallas as pl
from jax.experimental.pallas import tpu as pltpu

F32 = jnp.float32
BF16 = jnp.bfloat16

RMS_EPS = 1e-6
FFN_HALF = 0.5
N_SUB = 3
N_MOD = 3
N_MIXERS = 3
N_HEADS = 16
CONV_W = 3
POOL_WINDOWS = (2, 4, 8, 16)
POOL_HIST = max(POOL_WINDOWS) - 1
PAGE_SIZE = 128

SUBLANES = 8
LANES = 128
ROW_TILE = 512
ATTN_TILE = 256
VMEM_LIMIT = 56 * 1024 * 1024


def _params(*sem):
    return pltpu.CompilerParams(dimension_semantics=sem, vmem_limit_bytes=VMEM_LIMIT)


def _const_spec(shape):
    zeros = (0,) * len(shape)
    return pl.BlockSpec(shape, lambda *_: zeros)


def _rms(x, g):
    return x * lax.rsqrt(jnp.mean(x * x, axis=-1, keepdims=True) + RMS_EPS) * g


def _mod(mod_ref, j, m, per_row):
    k = j * N_MOD + m
    return mod_ref[k] if per_row else mod_ref[k:k + 1, :]


def _pre(x, mod_ref, g, j, per_row):
    return _rms(x, g) * (1.0 + _mod(mod_ref, j, 1, per_row)) + _mod(mod_ref, j, 0, per_row)


def _post(x, o, mod_ref, g, j, weight, per_row):
    return x + (weight * _mod(mod_ref, j, 2, per_row)) * _rms(o, g)


def _flat(x):
    return x.reshape(x.shape[0] * x.shape[1], x.shape[2]) if x.ndim == 3 else x


def _dot(a, b):
    return jnp.dot(a, b, preferred_element_type=F32)


def _softplus(z):
    return jnp.maximum(z, 0.0) + jnp.log(1.0 + jnp.exp(-jnp.abs(z)))


def _split_bf16(x):
    hi = x.astype(BF16)
    lo = (x - hi.astype(F32)).astype(BF16)
    return jnp.concatenate([hi, lo], axis=1)


def _ada_kernel(c_ref, w_ref, b_ref, o_ref):
    c = c_ref[...]
    c_act = (c * jax.nn.sigmoid(c)).astype(BF16)
    o_ref[...] = _dot(c_act, w_ref[...].astype(BF16)) + b_ref[...]


def _ada(c_all, w_ada, b_ada):
    depth, d, n = w_ada.shape
    n_comp = n // d
    rows = c_all.shape[0]
    return pl.pallas_call(
        _ada_kernel,
        grid=(depth, n_comp),
        in_specs=[
            _const_spec((rows, d)),
            pl.BlockSpec((None, d, d), lambda l, c: (l, 0, c)),
            pl.BlockSpec((None, None, 1, d), lambda l, c: (l, c, 0, 0)),
        ],
        out_specs=pl.BlockSpec((None, None, rows, d), lambda l, c: (l, c, 0, 0)),
        out_shape=jax.ShapeDtypeStruct((depth, n_comp, rows, d), F32),
        compiler_params=_params("arbitrary", "arbitrary"),
        name="ada_mod",
    )(c_all, w_ada, b_ada.reshape(depth, n_comp, 1, d))


class _Rows:
    def __init__(self, x, mod, layer):
        self.per_row = mod.shape[1] == N_SUB * N_MOD and mod.shape[2] != N_SUB * N_MOD
        d = x.shape[-1]
        if self.per_row:
            s, db, _ = x.shape
            self.grid = (1, 1)
            self.x_spec = pl.BlockSpec((s, db, d), lambda b, t: (0, 0, 0))
            self.mod_spec = pl.BlockSpec((None, N_SUB * N_MOD, db, d), lambda b, t: (layer, 0, 0, 0))
            self.tm = s * db
        else:
            bsz, t, _ = x.shape
            self.tm = min(ROW_TILE, t)
            self.grid = (bsz, t // self.tm)
            self.x_spec = pl.BlockSpec((None, self.tm, d), lambda b, t: (b, t, 0))
            self.mod_spec = pl.BlockSpec((None, None, N_SUB * N_MOD, d), lambda b, t: (layer, b, 0, 0))

    def like_x(self, width):
        shape = self.x_spec.block_shape[:-1] + (width,)
        return pl.BlockSpec(shape, self.x_spec.index_map)

    def tail_spec(self, rows, d):
        return pl.BlockSpec((None, rows, d), lambda b, t: (b, 0, 0))


def _w_spec(w):
    return pl.BlockSpec(w.shape, lambda b, t: (0,) * w.ndim, pipeline_mode=pl.Buffered(1))


def _ffn_kernel(x_ref, mod_ref, gpre_ref, gpost_ref, wg_ref, wu_ref, wd_ref, o_ref, *, j, per_row):
    x = x_ref[...]
    h = _flat(_pre(x, mod_ref, gpre_ref[...], j, per_row)).astype(BF16)
    g = _dot(h, wg_ref[...])
    u = _dot(h, wu_ref[...])
    a = ((g * jax.nn.sigmoid(g)) * u).astype(BF16)
    y = _dot(a, wd_ref[...]).reshape(x.shape)
    o_ref[...] = _post(x, y, mod_ref, gpost_ref[...], j, FFN_HALF, per_row)


def _ffn(x, mod, layer, j, gpre, gpost, wg, wu, wd):
    r = _Rows(x, mod, layer)
    d = x.shape[-1]
    return pl.pallas_call(
        functools.partial(_ffn_kernel, j=j, per_row=r.per_row),
        grid=r.grid,
        in_specs=[r.x_spec, r.mod_spec, _w_spec(gpre), _w_spec(gpost), _w_spec(wg), _w_spec(wu), _w_spec(wd)],
        out_specs=r.x_spec,
        out_shape=jax.ShapeDtypeStruct(x.shape, F32),
        compiler_params=_params("arbitrary", "arbitrary"),
        name="ffn",
    )(x, mod, gpre, gpost, wg, wu, wd)


def _qkv_kernel(x_ref, mod_ref, gpre_ref, w_ref, q_ref, k_ref, v_ref, kb_ref, vb_ref, *, per_row, q_scale):
    x = x_ref[...]
    d = x.shape[-1]
    h = _flat(_pre(x, mod_ref, gpre_ref[...], 1, per_row)).astype(BF16)
    qkv = _dot(h, w_ref[...])
    shape = q_ref.shape
    k = qkv[:, d:2 * d].reshape(shape)
    v = qkv[:, 2 * d:].reshape(shape)
    q_ref[...] = (qkv[:, :d] * q_scale).reshape(shape).astype(q_ref.dtype)
    k_ref[...] = k
    v_ref[...] = v
    kb_ref[...] = k.astype(BF16)
    vb_ref[...] = v.astype(BF16)


def _qkv(x, mod, layer, gpre, w, q_dtype):
    r = _Rows(x, mod, layer)
    d = x.shape[-1]
    spec = r.like_x(d)
    head_dim = d // N_HEADS
    return pl.pallas_call(
        functools.partial(_qkv_kernel, per_row=r.per_row, q_scale=head_dim ** -0.5),
        grid=r.grid,
        in_specs=[r.x_spec, r.mod_spec, _w_spec(gpre), _w_spec(w)],
        out_specs=[spec] * 5,
        out_shape=[jax.ShapeDtypeStruct(x.shape, dt) for dt in (q_dtype, F32, F32, BF16, BF16)],
        compiler_params=_params("arbitrary", "arbitrary"),
        name="qkv_proj",
    )(x, mod, gpre, w)


def _out_proj_kernel(x_ref, o_ref, mod_ref, gpost_ref, w_ref, y_ref, *, per_row):
    x = x_ref[...]
    y = _dot(_flat(o_ref[...]).astype(BF16), w_ref[...]).reshape(x.shape)
    y_ref[...] = _post(x, y, mod_ref, gpost_ref[...], 1, 1.0, per_row)


def _out_proj(x, o, mod, layer, gpost, w):
    r = _Rows(x, mod, layer)
    return pl.pallas_call(
        functools.partial(_out_proj_kernel, per_row=r.per_row),
        grid=r.grid,
        in_specs=[r.x_spec, r.x_spec, r.mod_spec, _w_spec(gpost), _w_spec(w)],
        out_specs=r.x_spec,
        out_shape=jax.ShapeDtypeStruct(x.shape, F32),
        compiler_params=_params("arbitrary", "arbitrary"),
        name="out_proj",
    )(x, o, mod, gpost, w)


def _suffix_ones(n):
    j = lax.broadcasted_iota(jnp.int32, (2 * n, n), 0) % n
    s = lax.broadcasted_iota(jnp.int32, (2 * n, n), 1)
    return (j >= s).astype(BF16)


def _dot_nt(a, b):
    return lax.dot_general(a, b, (((1,), (1,)), ((), ())), preferred_element_type=F32)


def _sb_block(qq, kb, vb, bias, uu, run_ref, acc_ref, mask, keys_on_lanes=False):
    z = (_dot(qq, kb) if keys_on_lanes else _dot_nt(qq, kb)) + bias
    sp = _softplus(z)
    if mask is not None:
        sp = jnp.where(mask, sp, 0.0)
    suffix = _dot(_split_bf16(sp), uu)
    run = run_ref[...]
    reps = z.shape[1] // run.shape[1]
    run_wide = run if reps == 1 else jnp.concatenate([run] * reps, axis=1)
    a = jnp.exp(z - suffix - run_wide)
    if mask is not None:
        a = jnp.where(mask, a, 0.0)
    a = a.astype(BF16)
    acc_ref[...] += _dot_nt(a, vb) if keys_on_lanes else _dot(a, vb)
    run_ref[...] = run + jnp.broadcast_to(suffix[:, 0:1], run.shape)


def _attn_prompt_kernel(bias_ref, q_ref, k_ref, v_ref, uu_ref, o_ref, run_ref, acc_ref, *, head_dim):
    hp = pl.program_id(1)
    i = pl.program_id(2)
    tq = q_ref.shape[0]
    q = q_ref[...]
    lane = lax.broadcasted_iota(jnp.int32, q.shape, 1)
    zero = jnp.zeros_like(q)
    qq = jnp.concatenate([jnp.where(lane < head_dim, q, zero), jnp.where(lane >= head_dim, q, zero)], axis=0)
    row = lax.broadcasted_iota(jnp.int32, (2 * tq, tq), 0)
    col = lax.broadcasted_iota(jnp.int32, (2 * tq, tq), 1)
    bias = jnp.where(row < tq, bias_ref[2 * hp], bias_ref[2 * hp + 1])
    mask = col < jnp.where(row < tq, row, row - tq)
    uu = uu_ref[...]
    run_ref[...] = jnp.zeros_like(run_ref)
    acc_ref[...] = jnp.zeros_like(acc_ref)

    def keys(j):
        start = pl.multiple_of(j * tq, tq)
        return k_ref[pl.ds(start, tq), :], v_ref[pl.ds(start, tq), :]

    _sb_block(qq, *keys(i), bias, uu, run_ref, acc_ref, mask)

    def body(n, carry):
        _sb_block(qq, *keys(i - 1 - n), bias, uu, run_ref, acc_ref, None)
        return carry

    lax.fori_loop(0, i, body, 0)
    acc = acc_ref[...]
    o_ref[...] = jnp.where(lane < head_dim, acc[:tq], acc[tq:]).astype(o_ref.dtype)


def _attn_prompt(q, kb, vb, bias):
    bsz, t, d = q.shape
    tq = min(ATTN_TILE, t)
    head_dim = d // N_HEADS
    pair = 2 * head_dim
    blk = pl.BlockSpec((None, tq, pair), lambda b, hp, i: (b, i, hp))
    seq = pl.BlockSpec((None, t, pair), lambda b, hp, i: (b, 0, hp))
    return pl.pallas_call(
        functools.partial(_attn_prompt_kernel, head_dim=head_dim),
        grid=(bsz, d // pair, t // tq),
        in_specs=[pl.BlockSpec(memory_space=pltpu.SMEM), blk, seq, seq,
                  pl.BlockSpec((2 * tq, tq), lambda b, hp, i: (0, 0))],
        out_specs=blk,
        out_shape=jax.ShapeDtypeStruct(q.shape, BF16),
        scratch_shapes=[pltpu.VMEM((2 * tq, LANES), F32), pltpu.VMEM((2 * tq, pair), F32)],
        compiler_params=_params("arbitrary", "arbitrary", "arbitrary"),
        name="sb_attn_prompt",
    )(bias, q, kb, vb, _suffix_ones(tq))


def _attn_sample_kernel(pt_ref, brow_ref, q_ref, kp_ref, vp_ref, kn_ref, vn_ref, uu_ref, o_ref,
                        qbd_ref, knew_ref, vnew_ref, run_ref, acc_ref, *, head_dim):
    p = pl.program_id(1)
    s, d = q_ref.shape
    rows = s * N_HEADS
    row = lax.broadcasted_iota(jnp.int32, (rows, d), 0)
    lane = lax.broadcasted_iota(jnp.int32, (rows, d), 1)
    own_head = (lane // head_dim) == (row % N_HEADS)
    bias = brow_ref[...]
    uu = uu_ref[...]

    @pl.when(p == 0)
    def _():
        q = q_ref[...]
        rep = jnp.concatenate([jnp.broadcast_to(q[t:t + 1, :], (N_HEADS, d)) for t in range(s)], axis=0)
        qbd_ref[...] = jnp.where(own_head, rep, 0.0).astype(BF16)
        run_ref[...] = jnp.zeros_like(run_ref)
        acc_ref[...] = jnp.zeros_like(acc_ref)
        knew_ref[...] = jnp.zeros_like(knew_ref)
        vnew_ref[...] = jnp.zeros_like(vnew_ref)
        knew_ref[0:kn_ref.shape[0], :] = kn_ref[...]
        vnew_ref[0:vn_ref.shape[0], :] = vn_ref[...]
        r2 = lax.broadcasted_iota(jnp.int32, (rows, PAGE_SIZE), 0)
        c2 = lax.broadcasted_iota(jnp.int32, (rows, PAGE_SIZE), 1)
        _sb_block(qbd_ref[...], knew_ref[...], vnew_ref[...], bias, uu, run_ref, acc_ref, c2 < r2 // N_HEADS)

    _sb_block(qbd_ref[...], kp_ref[...].astype(BF16), vp_ref[...].astype(BF16), bias, uu, run_ref, acc_ref, None,
              keys_on_lanes=True)

    @pl.when(p == pl.num_programs(1) - 1)
    def _():
        own = jnp.where(own_head, acc_ref[...], 0.0)
        o_ref[...] = jnp.sum(own.reshape(s, N_HEADS, d), axis=1)


def _attn_sample(q, k_new, v_new, cache_k, cache_v, page_table, bias, slot):
    db, s, d = q.shape
    n_pages = page_table.shape[1]
    head_dim = d // N_HEADS
    rows = s * N_HEADS
    pad = k_new.shape[1]
    bias_rows = jnp.broadcast_to(jnp.tile(bias, s)[:, None], (rows, PAGE_SIZE)).astype(F32)

    def page_map(b, p, pt):
        return (slot, pt[b * n_pages + n_pages - 1 - p], 0, 0)

    per_seq = lambda rws: pl.BlockSpec((None, rws, d), lambda b, p, pt: (b, 0, 0))
    grid_spec = pltpu.PrefetchScalarGridSpec(
        num_scalar_prefetch=1,
        grid=(db, n_pages),
        in_specs=[
            pl.BlockSpec((rows, PAGE_SIZE), lambda b, p, pt: (0, 0)),
            per_seq(s),
            pl.BlockSpec((None, None, d, PAGE_SIZE), page_map),
            pl.BlockSpec((None, None, d, PAGE_SIZE), page_map),
            per_seq(pad),
            per_seq(pad),
            pl.BlockSpec((2 * PAGE_SIZE, PAGE_SIZE), lambda b, p, pt: (0, 0)),
        ],
        out_specs=per_seq(s),
        scratch_shapes=[
            pltpu.VMEM((rows, d), BF16),
            pltpu.VMEM((PAGE_SIZE, d), BF16),
            pltpu.VMEM((PAGE_SIZE, d), BF16),
            pltpu.VMEM((rows, PAGE_SIZE), F32),
            pltpu.VMEM((rows, d), F32),
        ],
    )
    return pl.pallas_call(
        functools.partial(_attn_sample_kernel, head_dim=head_dim),
        grid_spec=grid_spec,
        out_shape=jax.ShapeDtypeStruct((db, s, d), F32),
        compiler_params=_params("arbitrary", "arbitrary"),
        name="sb_attn_sample",
    )(page_table.reshape(-1), bias_rows, q, cache_k, cache_v, k_new, v_new, _suffix_ones(PAGE_SIZE))


def _shift_rows(u, prev, s):
    if s == SUBLANES:
        return jnp.concatenate([prev, u[:-SUBLANES]], axis=0)
    ru = pltpu.roll(u, s, axis=0)
    rp = pltpu.roll(prev, s, axis=0)
    r8 = lax.broadcasted_iota(jnp.int32, prev.shape, 0)
    return jnp.concatenate([jnp.where(r8 < s, rp, ru[:SUBLANES]), ru[SUBLANES:]], axis=0)


def _conv_prompt_kernel(x_ref, mod_ref, gpre_ref, gpost_ref, wbcx_ref, cw_ref, wout_ref, y_ref, tail_ref, carry_ref):
    @pl.when(pl.program_id(1) == 0)
    def _():
        carry_ref[...] = jnp.zeros_like(carry_ref)

    x = x_ref[...]
    d = x.shape[-1]
    h = _pre(x, mod_ref, gpre_ref[...], 1, False).astype(BF16)
    bcx = _dot(h, wbcx_ref[...])
    u = bcx[:, d:2 * d] * bcx[:, 2 * d:]
    prev = carry_ref[...]
    y = cw_ref[0:1, :] * _shift_rows(u, prev, 2) + cw_ref[1:2, :] * _shift_rows(u, prev, 1) + cw_ref[2:3, :] * u
    last = u[u.shape[0] - SUBLANES:, :]
    carry_ref[...] = last
    tail_ref[...] = last
    o = _dot((bcx[:, :d] * y).astype(BF16), wout_ref[...])
    y_ref[...] = _post(x, o, mod_ref, gpost_ref[...], 1, 1.0, False)


def _conv_prompt(x, mod, layer, gpre, gpost, wbcx, cw, wout):
    r = _Rows(x, mod, layer)
    bsz, _, d = x.shape
    return pl.pallas_call(
        _conv_prompt_kernel,
        grid=r.grid,
        in_specs=[r.x_spec, r.mod_spec, _w_spec(gpre), _w_spec(gpost), _w_spec(wbcx), _w_spec(cw), _w_spec(wout)],
        out_specs=[r.x_spec, r.tail_spec(SUBLANES, d)],
        out_shape=[jax.ShapeDtypeStruct(x.shape, F32), jax.ShapeDtypeStruct((bsz, SUBLANES, d), F32)],
        scratch_shapes=[pltpu.VMEM((SUBLANES, d), F32)],
        compiler_params=_params("arbitrary", "arbitrary"),
        name="conv_prompt",
    )(x, mod, gpre, gpost, wbcx, cw, wout)


def _conv_sample_kernel(x_ref, st_ref, mod_ref, gpre_ref, gpost_ref, wbcx_ref, cw_ref, wout_ref, y_ref, ns_ref):
    x = x_ref[...]
    s, db, d = x.shape
    h = _flat(_pre(x, mod_ref, gpre_ref[...], 1, True)).astype(BF16)
    bcx = _dot(h, wbcx_ref[...])
    u = (bcx[:, d:2 * d] * bcx[:, 2 * d:]).reshape(s, db, d)
    ext = [st_ref[j] for j in range(CONV_W - 1)] + [u[t] for t in range(s)]
    y = jnp.stack([sum(cw_ref[j:j + 1, :] * ext[t + j] for j in range(CONV_W)) for t in range(s)], axis=0)
    for j in range(CONV_W - 1):
        ns_ref[j] = ext[s + j]
    o = _dot((bcx[:, :d] * _flat(y)).astype(BF16), wout_ref[...]).reshape(x.shape)
    y_ref[...] = _post(x, o, mod_ref, gpost_ref[...], 1, 1.0, True)


def _conv_sample(x, state, mod, layer, gpre, gpost, wbcx, cw, wout):
    r = _Rows(x, mod, layer)
    st_spec = pl.BlockSpec(state.shape, lambda b, t: (0, 0, 0))
    return pl.pallas_call(
        _conv_sample_kernel,
        grid=r.grid,
        in_specs=[r.x_spec, st_spec, r.mod_spec, _w_spec(gpre), _w_spec(gpost), _w_spec(wbcx), _w_spec(cw), _w_spec(wout)],
        out_specs=[r.x_spec, st_spec],
        out_shape=[jax.ShapeDtypeStruct(x.shape, F32), jax.ShapeDtypeStruct(state.shape, F32)],
        compiler_params=_params("arbitrary", "arbitrary"),
        name="conv_sample",
    )(x, state, mod, gpre, gpost, wbcx, cw, wout)


def _pool_mix(diff, wp_ref, scale):
    g = diff.shape[-1] // len(POOL_WINDOWS)
    parts = [_dot(diff[:, i * g:(i + 1) * g].astype(BF16), wp_ref[i]) for i in range(len(POOL_WINDOWS))]
    return jnp.concatenate(parts, axis=-1) * scale


def _pool_prompt_kernel(x_ref, mod_ref, gpre_ref, gpost_ref, wp_ref, ps_ref, y_ref, tail_ref, carry_ref):
    it = pl.program_id(1)

    @pl.when(it == 0)
    def _():
        carry_ref[...] = jnp.zeros_like(carry_ref)

    x = x_ref[...]
    tm, d = x.shape
    g = d // len(POOL_WINDOWS)
    h = _pre(x, mod_ref, gpre_ref[...], 1, False)
    sums = [h]
    for k in range(len(POOL_WINDOWS)):
        cur = sums[-1]
        sums.append(cur + _shift_rows(cur, carry_ref[k], 2 ** k))
        carry_ref[k] = cur[tm - SUBLANES:, :]
    assert POOL_WINDOWS == tuple(2 ** (k + 1) for k in range(len(POOL_WINDOWS)))
    total = jnp.concatenate([sums[k + 1][:, k * g:(k + 1) * g] for k in range(len(POOL_WINDOWS))], axis=-1)
    lane = lax.broadcasted_iota(jnp.int32, (tm, d), 1)
    pos = lax.broadcasted_iota(jnp.int32, (tm, d), 0) + it * tm
    window = jnp.left_shift(2, lane // g)
    count = jnp.minimum(pos + 1, window).astype(F32)
    diff = total / count - h
    tail_ref[...] = h[tm - 2 * SUBLANES:, :]
    y_ref[...] = _post(x, _pool_mix(diff, wp_ref, ps_ref[...]), mod_ref, gpost_ref[...], 1, 1.0, False)


def _pool_prompt(x, mod, layer, gpre, gpost, wp, ps):
    r = _Rows(x, mod, layer)
    bsz, _, d = x.shape
    return pl.pallas_call(
        _pool_prompt_kernel,
        grid=r.grid,
        in_specs=[r.x_spec, r.mod_spec, _w_spec(gpre), _w_spec(gpost), _w_spec(wp), _w_spec(ps)],
        out_specs=[r.x_spec, r.tail_spec(2 * SUBLANES, d)],
        out_shape=[jax.ShapeDtypeStruct(x.shape, F32), jax.ShapeDtypeStruct((bsz, 2 * SUBLANES, d), F32)],
        scratch_shapes=[pltpu.VMEM((len(POOL_WINDOWS), SUBLANES, d), F32)],
        compiler_params=_params("arbitrary", "arbitrary"),
        name="pool_prompt",
    )(x, mod, gpre, gpost, wp, ps)


def _pool_sample_kernel(x_ref, st_ref, mod_ref, gpre_ref, gpost_ref, wp_ref, ps_ref, y_ref, ns_ref, *, past_len):
    x = x_ref[...]
    s, db, d = x.shape
    g = d // len(POOL_WINDOWS)
    hist = st_ref.shape[0]
    h = _pre(x, mod_ref, gpre_ref[...], 1, True)
    ext = [st_ref[j] for j in range(hist)] + [h[t] for t in range(s)]
    means = []
    for t in range(s):
        parts = []
        for k, w in enumerate(POOL_WINDOWS):
            sl = slice(k * g, (k + 1) * g)
            tot = ext[hist + t][:, sl]
            for j in range(1, w):
                tot = tot + ext[hist + t - j][:, sl]
            parts.append(tot / float(min(past_len + t + 1, w)))
        means.append(jnp.concatenate(parts, axis=-1))
    diff = _flat(jnp.stack(means, axis=0) - h)
    for j in range(hist):
        ns_ref[j] = ext[s + j]
    mixed = _pool_mix(diff, wp_ref, ps_ref[...]).reshape(x.shape)
    y_ref[...] = _post(x, mixed, mod_ref, gpost_ref[...], 1, 1.0, True)


def _pool_sample(x, state, mod, layer, gpre, gpost, wp, ps, past_len):
    r = _Rows(x, mod, layer)
    st_spec = pl.BlockSpec(state.shape, lambda b, t: (0, 0, 0))
    return pl.pallas_call(
        functools.partial(_pool_sample_kernel, past_len=past_len),
        grid=r.grid,
        in_specs=[r.x_spec, st_spec, r.mod_spec, _w_spec(gpre), _w_spec(gpost), _w_spec(wp), _w_spec(ps)],
        out_specs=[r.x_spec, st_spec],
        out_shape=[jax.ShapeDtypeStruct(x.shape, F32), jax.ShapeDtypeStruct(state.shape, F32)],
        compiler_params=_params("arbitrary", "arbitrary"),
        name="pool_sample",
    )(x, state, mod, gpre, gpost, wp, ps)


def kernel(x_prompt, x_sample, cache_k, cache_v, state_conv, state_pool, page_table, c_prompt, c_sample,
           w_ada, b_ada, g_pre, g_post, w_ffn_gate, w_ffn_up, w_ffn_down, w_qkv, w_o, sb_bias, w_bcx,
           conv_w, w_conv_out, w_pool, pool_scale):
    bsz, seq, d = x_prompt.shape
    db, dec_seq, _ = x_sample.shape
    depth = w_ada.shape[0]
    head_dim = d // N_HEADS
    n_pages = page_table.shape[1]
    past_len = n_pages * PAGE_SIZE
    pad_new = 2 * SUBLANES
    assert dec_seq <= pad_new and db % SUBLANES == 0

    mod_all = _ada(jnp.concatenate([c_sample, c_prompt], axis=0), w_ada, b_ada)
    mod_p = jnp.swapaxes(mod_all[:, :, db:], 1, 2)

    wg, wu, wd = (w.astype(BF16) for w in (w_ffn_gate, w_ffn_up, w_ffn_down))
    wqkv, wo, wbcx, wco, wpl = (w.astype(BF16) for w in (w_qkv, w_o, w_bcx, w_conv_out, w_pool))
    pages = lambda c: jnp.transpose(c, (0, 1, 3, 4, 2)).reshape(c.shape[0], c.shape[1], d, PAGE_SIZE)
    cache_k4, cache_v4 = pages(cache_k), pages(cache_v)

    xp = x_prompt
    xs = jnp.swapaxes(x_sample, 0, 1)
    st_conv = jnp.swapaxes(state_conv, 1, 2)
    st_pool = jnp.swapaxes(state_pool, 1, 2)

    k_p, v_p, conv_p, pool_p, k_s, v_s, conv_s, pool_s = ([] for _ in range(8))
    for i in range(depth):
        gp = lambda j: g_pre[i, j][None, :]
        gq = lambda j: g_post[i, j][None, :]
        kind, slot = i % N_MIXERS, i // N_MIXERS

        xp = _ffn(xp, mod_p, i, 0, gp(0), gq(0), wg[i, 0], wu[i, 0], wd[i, 0])
        xs = _ffn(xs, mod_all, i, 0, gp(0), gq(0), wg[i, 0], wu[i, 0], wd[i, 0])

        if kind == 0:
            q, k, v, kb, vb = _qkv(xp, mod_p, i, gp(1), wqkv[slot], BF16)
            o = _attn_prompt(q, kb, vb, sb_bias[slot])
            xp = _out_proj(xp, o, mod_p, i, gq(1), wo[slot])
            k_p.append(k.reshape(bsz, seq, N_HEADS, head_dim))
            v_p.append(v.reshape(bsz, seq, N_HEADS, head_dim))

            q, k, v, kb, vb = _qkv(xs, mod_all, i, gp(1), wqkv[slot], F32)
            to_seq = lambda a: jnp.swapaxes(a, 0, 1)
            pad = lambda a: jnp.pad(to_seq(a), ((0, 0), (0, pad_new - dec_seq), (0, 0)))
            o = _attn_sample(to_seq(q), pad(kb), pad(vb), cache_k4, cache_v4, page_table, sb_bias[slot], slot)
            xs = _out_proj(xs, to_seq(o), mod_all, i, gq(1), wo[slot])
            k_s.append(to_seq(k).reshape(db, dec_seq, N_HEADS, head_dim))
            v_s.append(to_seq(v).reshape(db, dec_seq, N_HEADS, head_dim))
        elif kind == 1:
            xp, tail = _conv_prompt(xp, mod_p, i, gp(1), gq(1), wbcx[slot], conv_w[slot], wco[slot])
            conv_p.append(tail[:, SUBLANES - (CONV_W - 1):])
            xs, ns = _conv_sample(xs, st_conv[slot], mod_all, i, gp(1), gq(1), wbcx[slot], conv_w[slot], wco[slot])
            conv_s.append(jnp.swapaxes(ns, 0, 1))
        else:
            ps = pool_scale[slot][None, :]
            xp, tail = _pool_prompt(xp, mod_p, i, gp(1), gq(1), wpl[slot], ps)
            pool_p.append(tail[:, 2 * SUBLANES - POOL_HIST:])
            xs, ns = _pool_sample(xs, st_pool[slot], mod_all, i, gp(1), gq(1), wpl[slot], ps, past_len)
            pool_s.append(jnp.swapaxes(ns, 0, 1))

        xp = _ffn(xp, mod_p, i, 2, gp(2), gq(2), wg[i, 1], wu[i, 1], wd[i, 1])
        xs = _ffn(xs, mod_all, i, 2, gp(2), gq(2), wg[i, 1], wu[i, 1], wd[i, 1])

    return (xp, jnp.swapaxes(xs, 0, 1), jnp.stack(k_p), jnp.stack(v_p), jnp.stack(conv_p), jnp.stack(pool_p),
            jnp.stack(k_s), jnp.stack(v_s), jnp.stack(conv_s), jnp.stack(pool_s))
```

```python
import functools

import jax
import jax.numpy as jnp
from jax import lax
from jax.experimental import pallas as pl
from jax.experimental.pallas import tpu as pltpu

F32 = jnp.float32
BF16 = jnp.bfloat16

RMS_EPS = 1e-6
FFN_HALF = 0.5
N_SUB = 3
N_MOD = 3
N_MIXERS = 3
N_HEADS = 16
CONV_W = 3
POOL_WINDOWS = (2, 4, 8, 16)
POOL_HIST = max(POOL_WINDOWS) - 1
PAGE_SIZE = 128

SUBLANES = 8
LANES = 128
ROW_TILE = 512
ATTN_TILE = 512
ATTN_SUB = 256
SAMPLE_PAGES_PER_STEP = 8
VMEM_LIMIT = 56 * 1024 * 1024


def _params(*sem):
    return pltpu.CompilerParams(dimension_semantics=sem, vmem_limit_bytes=VMEM_LIMIT)


def _const_spec(shape):
    zeros = (0,) * len(shape)
    return pl.BlockSpec(shape, lambda *_: zeros)


def _rms(x, g):
    return x * lax.rsqrt(jnp.mean(x * x, axis=-1, keepdims=True) + RMS_EPS) * g


def _mod(mod_ref, j, m, per_row):
    k = j * N_MOD + m
    return mod_ref[k] if per_row else mod_ref[k:k + 1, :]


def _pre(x, mod_ref, g, j, per_row):
    return _rms(x, g) * (1.0 + _mod(mod_ref, j, 1, per_row)) + _mod(mod_ref, j, 0, per_row)


def _post(x, o, mod_ref, g, j, weight, per_row):
    return x + (weight * _mod(mod_ref, j, 2, per_row)) * _rms(o, g)


def _flat(x):
    return x.reshape(x.shape[0] * x.shape[1], x.shape[2]) if x.ndim == 3 else x


def _dot(a, b):
    return jnp.dot(a, b, preferred_element_type=F32)


def _softplus(z):
    return jnp.maximum(z, 0.0) + jnp.log(1.0 + jnp.exp(-jnp.abs(z)))


def _split_bf16(x):
    hi = x.astype(BF16)
    lo = (x - hi.astype(F32)).astype(BF16)
    return jnp.concatenate([hi, lo], axis=1)


def _ada_kernel(c_ref, w_ref, b_ref, o_ref):
    c = c_ref[...]
    c_act = (c * jax.nn.sigmoid(c)).astype(BF16)
    o_ref[...] = _dot(c_act, w_ref[...].astype(BF16)) + b_ref[...]


def _ada(c_all, w_ada, b_ada):
    depth, d, n = w_ada.shape
    n_comp = n // d
    rows = c_all.shape[0]
    return pl.pallas_call(
        _ada_kernel,
        grid=(depth, n_comp),
        in_specs=[
            _const_spec((rows, d)),
            pl.BlockSpec((None, d, d), lambda l, c: (l, 0, c)),
            pl.BlockSpec((None, None, 1, d), lambda l, c: (l, c, 0, 0)),
        ],
        out_specs=pl.BlockSpec((None, None, rows, d), lambda l, c: (l, c, 0, 0)),
        out_shape=jax.ShapeDtypeStruct((depth, n_comp, rows, d), F32),
        compiler_params=_params("arbitrary", "arbitrary"),
        name="ada_mod",
    )(c_all, w_ada, b_ada.reshape(depth, n_comp, 1, d))


class _Rows:
    def __init__(self, x, mod, layer):
        self.per_row = mod.shape[1] == N_SUB * N_MOD and mod.shape[2] != N_SUB * N_MOD
        d = x.shape[-1]
        if self.per_row:
            s, db, _ = x.shape
            self.grid = (1, 1)
            self.x_spec = pl.BlockSpec((s, db, d), lambda b, t: (0, 0, 0))
            self.mod_spec = pl.BlockSpec((None, N_SUB * N_MOD, db, d), lambda b, t: (layer, 0, 0, 0))
            self.tm = s * db
        else:
            bsz, t, _ = x.shape
            self.tm = min(ROW_TILE, t)
            self.grid = (bsz, t // self.tm)
            self.x_spec = pl.BlockSpec((None, self.tm, d), lambda b, t: (b, t, 0))
            self.mod_spec = pl.BlockSpec((None, None, N_SUB * N_MOD, d), lambda b, t: (layer, b, 0, 0))

    def like_x(self, width):
        shape = self.x_spec.block_shape[:-1] + (width,)
        return pl.BlockSpec(shape, self.x_spec.index_map)

    def tail_spec(self, rows, d):
        return pl.BlockSpec((None, rows, d), lambda b, t: (b, 0, 0))


class _At:
    def __init__(self, array, *lead):
        self.array, self.lead = array, lead


def _w_spec(w):
    shape = w.array.shape
    block = (None,) * len(w.lead) + shape[len(w.lead):]
    index = tuple(w.lead) + (0,) * (len(shape) - len(w.lead))
    return pl.BlockSpec(block, lambda *_: index, pipeline_mode=pl.Buffered(1))


def _ffn_kernel(x_ref, mod_ref, gpre_ref, gpost_ref, wg_ref, wu_ref, wd_ref, o_ref, *, j, per_row):
    x = x_ref[...]
    h = _flat(_pre(x, mod_ref, gpre_ref[...], j, per_row)).astype(BF16)
    g = _dot(h, wg_ref[...])
    u = _dot(h, wu_ref[...])
    a = ((g * jax.nn.sigmoid(g)) * u).astype(BF16)
    y = _dot(a, wd_ref[...]).reshape(x.shape)
    o_ref[...] = _post(x, y, mod_ref, gpost_ref[...], j, FFN_HALF, per_row)


def _ffn(x, mod, layer, j, gpre, gpost, wg, wu, wd):
    r = _Rows(x, mod, layer)
    d = x.shape[-1]
    return pl.pallas_call(
        functools.partial(_ffn_kernel, j=j, per_row=r.per_row),
        grid=r.grid,
        in_specs=[r.x_spec, r.mod_spec, _w_spec(gpre), _w_spec(gpost), _w_spec(wg), _w_spec(wu), _w_spec(wd)],
        out_specs=r.x_spec,
        out_shape=jax.ShapeDtypeStruct(x.shape, F32),
        compiler_params=_params("arbitrary", "arbitrary"),
        name="ffn",
    )(x, mod, gpre.array, gpost.array, wg.array, wu.array, wd.array)


def _qkv_kernel(x_ref, mod_ref, gpre_ref, w_ref, q_ref, k_ref, v_ref, kb_ref, vb_ref, *, per_row, q_scale):
    x = x_ref[...]
    d = x.shape[-1]
    h = _flat(_pre(x, mod_ref, gpre_ref[...], 1, per_row)).astype(BF16)
    qkv = _dot(h, w_ref[...])
    shape = q_ref.shape
    k = qkv[:, d:2 * d].reshape(shape)
    v = qkv[:, 2 * d:].reshape(shape)
    q_ref[...] = (qkv[:, :d] * q_scale).reshape(shape).astype(q_ref.dtype)
    k_ref[...] = k
    v_ref[...] = v
    kb_ref[...] = k.astype(BF16)
    vb_ref[...] = v.astype(BF16)


def _qkv(x, mod, layer, gpre, w, q_dtype):
    r = _Rows(x, mod, layer)
    d = x.shape[-1]
    spec = r.like_x(d)
    head_dim = d // N_HEADS
    return pl.pallas_call(
        functools.partial(_qkv_kernel, per_row=r.per_row, q_scale=head_dim ** -0.5),
        grid=r.grid,
        in_specs=[r.x_spec, r.mod_spec, _w_spec(gpre), _w_spec(w)],
        out_specs=[spec] * 5,
        out_shape=[jax.ShapeDtypeStruct(x.shape, dt) for dt in (q_dtype, F32, F32, BF16, BF16)],
        compiler_params=_params("arbitrary", "arbitrary"),
        name="qkv_proj",
    )(x, mod, gpre.array, w.array)


def _out_proj_kernel(x_ref, o_ref, mod_ref, gpost_ref, w_ref, y_ref, *, per_row):
    x = x_ref[...]
    y = _dot(_flat(o_ref[...]).astype(BF16), w_ref[...]).reshape(x.shape)
    y_ref[...] = _post(x, y, mod_ref, gpost_ref[...], 1, 1.0, per_row)


def _out_proj(x, o, mod, layer, gpost, w):
    r = _Rows(x, mod, layer)
    return pl.pallas_call(
        functools.partial(_out_proj_kernel, per_row=r.per_row),
        grid=r.grid,
        in_specs=[r.x_spec, r.x_spec, r.mod_spec, _w_spec(gpost), _w_spec(w)],
        out_specs=r.x_spec,
        out_shape=jax.ShapeDtypeStruct(x.shape, F32),
        compiler_params=_params("arbitrary", "arbitrary"),
        name="out_proj",
    )(x, o, mod, gpost.array, w.array)


def _suffix_ones(n):
    j = lax.broadcasted_iota(jnp.int32, (2 * n, n), 0) % n
    s = lax.broadcasted_iota(jnp.int32, (2 * n, n), 1)
    return (j >= s).astype(BF16)


def _dot_nt(a, b):
    return lax.dot_general(a, b, (((1,), (1,)), ((), ())), preferred_element_type=F32)


def _sb_weights(z, uu, run_ref, mask):
    sp = _softplus(z)
    if mask is not None:
        sp = jnp.where(mask, sp, 0.0)
    sub = uu.shape[1]
    run = run_ref[...]
    reps = sub // run.shape[1]
    args = []
    for s in reversed(range(z.shape[1] // sub)):
        cols = slice(s * sub, (s + 1) * sub)
        suffix = _dot(_split_bf16(sp[:, cols]), uu)
        run_wide = run if reps == 1 else jnp.concatenate([run] * reps, axis=1)
        args.insert(0, z[:, cols] - suffix - run_wide)
        run = run + jnp.broadcast_to(suffix[:, 0:1], run.shape)
    a = jnp.exp(args[0] if len(args) == 1 else jnp.concatenate(args, axis=1))
    if mask is not None:
        a = jnp.where(mask, a, 0.0)
    run_ref[...] = run
    return a.astype(BF16)


def _attn_prompt_kernel(bias_ref, q_ref, k_ref, v_ref, uu_ref, o_ref, run_ref, acc_ref, *, head_dim):
    hp = pl.program_id(1)
    i = pl.program_id(2)
    tq = q_ref.shape[0]
    q = q_ref[...]
    lane = lax.broadcasted_iota(jnp.int32, q.shape, 1)
    zero = jnp.zeros_like(q)
    qq = jnp.concatenate([jnp.where(lane < head_dim, q, zero), jnp.where(lane >= head_dim, q, zero)], axis=0)
    row = lax.broadcasted_iota(jnp.int32, (2 * tq, tq), 0)
    col = lax.broadcasted_iota(jnp.int32, (2 * tq, tq), 1)
    mask = col < jnp.where(row < tq, row, row - tq)
    b0 = bias_ref[2 * hp]
    b1 = bias_ref[2 * hp + 1]
    add_bias = lambda z: jnp.concatenate([z[:tq] + b0, z[tq:] + b1], axis=0)
    uu = uu_ref[...]
    run_ref[...] = jnp.zeros_like(run_ref)
    acc_ref[...] = jnp.zeros_like(acc_ref)

    def block(j, mask):
        keys = pl.ds(pl.multiple_of(j * tq, tq), tq)
        a = _sb_weights(add_bias(_dot_nt(qq, k_ref[keys, :])), uu, run_ref, mask)
        acc_ref[...] += _dot(a, v_ref[keys, :])

    block(i, mask)

    def body(n, carry):
        block(i - 1 - n, None)
        return carry

    lax.fori_loop(0, i, body, 0)
    acc = acc_ref[...]
    o_ref[...] = jnp.where(lane < head_dim, acc[:tq], acc[tq:]).astype(o_ref.dtype)


def _attn_prompt(q, kb, vb, bias):
    bsz, t, d = q.shape
    tq = min(ATTN_TILE, t)
    sub = min(ATTN_SUB, tq)
    head_dim = d // N_HEADS
    pair = 2 * head_dim
    blk = pl.BlockSpec((None, tq, pair), lambda b, hp, i: (b, i, hp))
    seq = pl.BlockSpec((None, t, pair), lambda b, hp, i: (b, 0, hp))
    return pl.pallas_call(
        functools.partial(_attn_prompt_kernel, head_dim=head_dim),
        grid=(bsz, d // pair, t // tq),
        in_specs=[pl.BlockSpec(memory_space=pltpu.SMEM), blk, seq, seq,
                  pl.BlockSpec((2 * sub, sub), lambda b, hp, i: (0, 0))],
        out_specs=blk,
        out_shape=jax.ShapeDtypeStruct(q.shape, BF16),
        scratch_shapes=[pltpu.VMEM((2 * tq, LANES), F32), pltpu.VMEM((2 * tq, pair), F32)],
        compiler_params=_params("arbitrary", "arbitrary", "arbitrary"),
        name="sb_attn_prompt",
    )(bias, q, kb, vb, _suffix_ones(sub))


def _attn_sample_kernel(pt_ref, brow_ref, q_ref, k_ref, v_ref, *rest, head_dim, pages_per_step):
    pg = pages_per_step
    kp_refs, vp_refs = rest[:pg], rest[pg:2 * pg]
    uu_ref, o_ref, qbd_ref, knew_ref, vnew_ref, run_ref, acc_ref = rest[2 * pg:]
    b = pl.program_id(0)
    p = pl.program_id(1)
    s, _, d = q_ref.shape
    rows = s * N_HEADS
    row = lax.broadcasted_iota(jnp.int32, (rows, d), 0)
    lane = lax.broadcasted_iota(jnp.int32, (rows, d), 1)
    own_head = (lane // head_dim) == (row % N_HEADS)
    bias = brow_ref[...]
    uu = uu_ref[...]

    @pl.when(p == 0)
    def _():
        rep = jnp.concatenate([jnp.broadcast_to(q_ref[t, pl.ds(b, 1), :], (N_HEADS, d)) for t in range(s)], axis=0)
        qbd_ref[...] = jnp.where(own_head, rep, 0.0).astype(BF16)
        run_ref[...] = jnp.zeros_like(run_ref)
        acc_ref[...] = jnp.zeros_like(acc_ref)
        pad = 2 * SUBLANES
        r16 = lax.broadcasted_iota(jnp.int32, (pad, d), 0)
        kn = jnp.zeros((pad, d), F32)
        vn = jnp.zeros((pad, d), F32)
        for t in range(s):
            kn = jnp.where(r16 == t, jnp.broadcast_to(k_ref[t, pl.ds(b, 1), :], (pad, d)), kn)
            vn = jnp.where(r16 == t, jnp.broadcast_to(v_ref[t, pl.ds(b, 1), :], (pad, d)), vn)
        knew_ref[...] = jnp.zeros_like(knew_ref)
        vnew_ref[...] = jnp.zeros_like(vnew_ref)
        knew_ref[0:pad, :] = kn.astype(BF16)
        vnew_ref[0:pad, :] = vn.astype(BF16)
        r2 = lax.broadcasted_iota(jnp.int32, (rows, PAGE_SIZE), 0)
        c2 = lax.broadcasted_iota(jnp.int32, (rows, PAGE_SIZE), 1)
        a = _sb_weights(_dot_nt(qbd_ref[...], knew_ref[...]) + bias, uu, run_ref, c2 < r2 // N_HEADS)
        acc_ref[...] += _dot(a, vnew_ref[...])

    qbd = qbd_ref[...]
    z = jnp.concatenate([_dot(qbd, kp_refs[pg - 1 - c][...].astype(BF16)) + bias for c in range(pg)], axis=1)
    a = _sb_weights(z, uu, run_ref, None)
    acc = acc_ref[...]
    for c in range(pg):
        acc = acc + _dot_nt(a[:, c * PAGE_SIZE:(c + 1) * PAGE_SIZE], vp_refs[pg - 1 - c][...].astype(BF16))
    acc_ref[...] = acc

    @pl.when(p == pl.num_programs(1) - 1)
    def _():
        own = jnp.where(own_head, acc_ref[...], 0.0)
        o_ref[...] = jnp.sum(own.reshape(s, N_HEADS, d), axis=1)


def _attn_sample(q, k_new, v_new, cache_k, cache_v, page_table, bias, slot):
    s, db, d = q.shape
    n_pages = page_table.shape[1]
    head_dim = d // N_HEADS
    rows = s * N_HEADS
    pg = max(g for g in range(1, SAMPLE_PAGES_PER_STEP + 1) if n_pages % g == 0)
    bias_rows = jnp.broadcast_to(jnp.tile(bias, s)[:, None], (rows, PAGE_SIZE)).astype(F32)

    def page_spec(g):
        def page_map(b, p, pt):
            return (slot, pt[b * n_pages + n_pages - 1 - (p * pg + g)], 0, 0)
        return pl.BlockSpec((None, None, d, PAGE_SIZE), page_map)

    const = lambda shape: pl.BlockSpec(shape, lambda b, p, pt: (0,) * len(shape), pipeline_mode=pl.Buffered(1))
    grid_spec = pltpu.PrefetchScalarGridSpec(
        num_scalar_prefetch=1,
        grid=(db, n_pages // pg),
        in_specs=[const((rows, PAGE_SIZE)), const(q.shape), const(q.shape), const(q.shape)]
        + [page_spec(g) for g in range(pg)] * 2
        + [const((2 * PAGE_SIZE, PAGE_SIZE))],
        out_specs=pl.BlockSpec((None, s, d), lambda b, p, pt: (b, 0, 0)),
        scratch_shapes=[
            pltpu.VMEM((rows, d), BF16),
            pltpu.VMEM((PAGE_SIZE, d), BF16),
            pltpu.VMEM((PAGE_SIZE, d), BF16),
            pltpu.VMEM((rows, PAGE_SIZE), F32),
            pltpu.VMEM((rows, d), F32),
        ],
    )
    return pl.pallas_call(
        functools.partial(_attn_sample_kernel, head_dim=head_dim, pages_per_step=pg),
        grid_spec=grid_spec,
        out_shape=jax.ShapeDtypeStruct((db, s, d), F32),
        compiler_params=_params("arbitrary", "arbitrary"),
        name="sb_attn_sample",
    )(page_table.reshape(-1), bias_rows, q, k_new, v_new, *([cache_k] * pg), *([cache_v] * pg),
      _suffix_ones(PAGE_SIZE))


def _shift_rows(u, prev, s):
    if s == SUBLANES:
        return jnp.concatenate([prev, u[:-SUBLANES]], axis=0)
    ru = pltpu.roll(u, s, axis=0)
    rp = pltpu.roll(prev, s, axis=0)
    r8 = lax.broadcasted_iota(jnp.int32, prev.shape, 0)
    return jnp.concatenate([jnp.where(r8 < s, rp, ru[:SUBLANES]), ru[SUBLANES:]], axis=0)


def _conv_prompt_kernel(x_ref, mod_ref, gpre_ref, gpost_ref, wbcx_ref, cw_ref, wout_ref, y_ref, tail_ref, carry_ref):
    @pl.when(pl.program_id(1) == 0)
    def _():
        carry_ref[...] = jnp.zeros_like(carry_ref)

    x = x_ref[...]
    d = x.shape[-1]
    h = _pre(x, mod_ref, gpre_ref[...], 1, False).astype(BF16)
    bcx = _dot(h, wbcx_ref[...])
    u = bcx[:, d:2 * d] * bcx[:, 2 * d:]
    prev = carry_ref[...]
    y = cw_ref[0:1, :] * _shift_rows(u, prev, 2) + cw_ref[1:2, :] * _shift_rows(u, prev, 1) + cw_ref[2:3, :] * u
    last = u[u.shape[0] - SUBLANES:, :]
    carry_ref[...] = last
    tail_ref[...] = last
    o = _dot((bcx[:, :d] * y).astype(BF16), wout_ref[...])
    y_ref[...] = _post(x, o, mod_ref, gpost_ref[...], 1, 1.0, False)


def _conv_prompt(x, mod, layer, gpre, gpost, wbcx, cw, wout):
    r = _Rows(x, mod, layer)
    bsz, _, d = x.shape
    return pl.pallas_call(
        _conv_prompt_kernel,
        grid=r.grid,
        in_specs=[r.x_spec, r.mod_spec, _w_spec(gpre), _w_spec(gpost), _w_spec(wbcx), _w_spec(cw), _w_spec(wout)],
        out_specs=[r.x_spec, r.tail_spec(SUBLANES, d)],
        out_shape=[jax.ShapeDtypeStruct(x.shape, F32), jax.ShapeDtypeStruct((bsz, SUBLANES, d), F32)],
        scratch_shapes=[pltpu.VMEM((SUBLANES, d), F32)],
        compiler_params=_params("arbitrary", "arbitrary"),
        name="conv_prompt",
    )(x, mod, gpre.array, gpost.array, wbcx.array, cw.array, wout.array)


def _conv_sample_kernel(x_ref, st_ref, mod_ref, gpre_ref, gpost_ref, wbcx_ref, cw_ref, wout_ref, y_ref, ns_ref):
    x = x_ref[...]
    s, db, d = x.shape
    h = _flat(_pre(x, mod_ref, gpre_ref[...], 1, True)).astype(BF16)
    bcx = _dot(h, wbcx_ref[...])
    u = (bcx[:, d:2 * d] * bcx[:, 2 * d:]).reshape(s, db, d)
    ext = [st_ref[j] for j in range(CONV_W - 1)] + [u[t] for t in range(s)]
    y = jnp.stack([sum(cw_ref[j:j + 1, :] * ext[t + j] for j in range(CONV_W)) for t in range(s)], axis=0)
    for j in range(CONV_W - 1):
        ns_ref[j] = ext[s + j]
    o = _dot((bcx[:, :d] * _flat(y)).astype(BF16), wout_ref[...]).reshape(x.shape)
    y_ref[...] = _post(x, o, mod_ref, gpost_ref[...], 1, 1.0, True)


def _conv_sample(x, state, mod, layer, gpre, gpost, wbcx, cw, wout):
    r = _Rows(x, mod, layer)
    st_spec = pl.BlockSpec(state.shape, lambda b, t: (0, 0, 0))
    return pl.pallas_call(
        _conv_sample_kernel,
        grid=r.grid,
        in_specs=[r.x_spec, st_spec, r.mod_spec, _w_spec(gpre), _w_spec(gpost), _w_spec(wbcx), _w_spec(cw), _w_spec(wout)],
        out_specs=[r.x_spec, st_spec],
        out_shape=[jax.ShapeDtypeStruct(x.shape, F32), jax.ShapeDtypeStruct(state.shape, F32)],
        compiler_params=_params("arbitrary", "arbitrary"),
        name="conv_sample",
    )(x, state, mod, gpre.array, gpost.array, wbcx.array, cw.array, wout.array)


def _pool_mix(diff, wp_ref, scale):
    g = diff.shape[-1] // len(POOL_WINDOWS)
    parts = [_dot(diff[:, i * g:(i + 1) * g].astype(BF16), wp_ref[i]) for i in range(len(POOL_WINDOWS))]
    return jnp.concatenate(parts, axis=-1) * scale


def _pool_prompt_kernel(x_ref, mod_ref, gpre_ref, gpost_ref, wp_ref, ps_ref, y_ref, tail_ref, carry_ref):
    it = pl.program_id(1)

    @pl.when(it == 0)
    def _():
        carry_ref[...] = jnp.zeros_like(carry_ref)

    x = x_ref[...]
    tm, d = x.shape
    g = d // len(POOL_WINDOWS)
    h = _pre(x, mod_ref, gpre_ref[...], 1, False)
    sums = [h]
    for k in range(len(POOL_WINDOWS)):
        cur = sums[-1]
        sums.append(cur + _shift_rows(cur, carry_ref[k], 2 ** k))
        carry_ref[k] = cur[tm - SUBLANES:, :]
    assert POOL_WINDOWS == tuple(2 ** (k + 1) for k in range(len(POOL_WINDOWS)))
    total = jnp.concatenate([sums[k + 1][:, k * g:(k + 1) * g] for k in range(len(POOL_WINDOWS))], axis=-1)
    lane = lax.broadcasted_iota(jnp.int32, (tm, d), 1)
    pos = lax.broadcasted_iota(jnp.int32, (tm, d), 0) + it * tm
    window = jnp.left_shift(2, lane // g)
    count = jnp.minimum(pos + 1, window).astype(F32)
    diff = total / count - h
    tail_ref[...] = h[tm - 2 * SUBLANES:, :]
    y_ref[...] = _post(x, _pool_mix(diff, wp_ref, ps_ref[...]), mod_ref, gpost_ref[...], 1, 1.0, False)


def _pool_prompt(x, mod, layer, gpre, gpost, wp, ps):
    r = _Rows(x, mod, layer)
    bsz, _, d = x.shape
    return pl.pallas_call(
        _pool_prompt_kernel,
        grid=r.grid,
        in_specs=[r.x_spec, r.mod_spec, _w_spec(gpre), _w_spec(gpost), _w_spec(wp), _w_spec(ps)],
        out_specs=[r.x_spec, r.tail_spec(2 * SUBLANES, d)],
        out_shape=[jax.ShapeDtypeStruct(x.shape, F32), jax.ShapeDtypeStruct((bsz, 2 * SUBLANES, d), F32)],
        scratch_shapes=[pltpu.VMEM((len(POOL_WINDOWS), SUBLANES, d), F32)],
        compiler_params=_params("arbitrary", "arbitrary"),
        name="pool_prompt",
    )(x, mod, gpre.array, gpost.array, wp.array, ps.array)


def _pool_sample_kernel(x_ref, st_ref, mod_ref, gpre_ref, gpost_ref, wp_ref, ps_ref, y_ref, ns_ref, *, past_len):
    x = x_ref[...]
    s, db, d = x.shape
    g = d // len(POOL_WINDOWS)
    hist = st_ref.shape[0]
    h = _pre(x, mod_ref, gpre_ref[...], 1, True)
    ext = [st_ref[j] for j in range(hist)] + [h[t] for t in range(s)]
    means = []
    for t in range(s):
        parts = []
        for k, w in enumerate(POOL_WINDOWS):
            sl = slice(k * g, (k + 1) * g)
            tot = ext[hist + t][:, sl]
            for j in range(1, w):
                tot = tot + ext[hist + t - j][:, sl]
            parts.append(tot / float(min(past_len + t + 1, w)))
        means.append(jnp.concatenate(parts, axis=-1))
    diff = _flat(jnp.stack(means, axis=0) - h)
    for j in range(hist):
        ns_ref[j] = ext[s + j]
    mixed = _pool_mix(diff, wp_ref, ps_ref[...]).reshape(x.shape)
    y_ref[...] = _post(x, mixed, mod_ref, gpost_ref[...], 1, 1.0, True)


def _pool_sample(x, state, mod, layer, gpre, gpost, wp, ps, past_len):
    r = _Rows(x, mod, layer)
    st_spec = pl.BlockSpec(state.shape, lambda b, t: (0, 0, 0))
    return pl.pallas_call(
        functools.partial(_pool_sample_kernel, past_len=past_len),
        grid=r.grid,
        in_specs=[r.x_spec, st_spec, r.mod_spec, _w_spec(gpre), _w_spec(gpost), _w_spec(wp), _w_spec(ps)],
        out_specs=[r.x_spec, st_spec],
        out_shape=[jax.ShapeDtypeStruct(x.shape, F32), jax.ShapeDtypeStruct(state.shape, F32)],
        compiler_params=_params("arbitrary", "arbitrary"),
        name="pool_sample",
    )(x, state, mod, gpre.array, gpost.array, wp.array, ps.array)


def kernel(x_prompt, x_sample, cache_k, cache_v, state_conv, state_pool, page_table, c_prompt, c_sample,
           w_ada, b_ada, g_pre, g_post, w_ffn_gate, w_ffn_up, w_ffn_down, w_qkv, w_o, sb_bias, w_bcx,
           conv_w, w_conv_out, w_pool, pool_scale):
    bsz, seq, d = x_prompt.shape
    db, dec_seq, _ = x_sample.shape
    depth = w_ada.shape[0]
    head_dim = d // N_HEADS
    n_pages = page_table.shape[1]
    past_len = n_pages * PAGE_SIZE
    assert dec_seq <= 2 * SUBLANES and db % SUBLANES == 0

    mod_all = _ada(jnp.concatenate([c_sample, c_prompt], axis=0), w_ada, b_ada)
    mod_p = jnp.swapaxes(mod_all[:, :, db:], 1, 2)

    wg, wu, wd = (w.astype(BF16) for w in (w_ffn_gate, w_ffn_up, w_ffn_down))
    wqkv, wo, wbcx, wco, wpl = (w.astype(BF16) for w in (w_qkv, w_o, w_bcx, w_conv_out, w_pool))
    pages = lambda c: jnp.transpose(c, (0, 1, 3, 4, 2)).reshape(c.shape[0], c.shape[1], d, PAGE_SIZE)
    cache_k4, cache_v4 = pages(cache_k), pages(cache_v)

    xp = x_prompt
    xs = jnp.swapaxes(x_sample, 0, 1)
    st_conv = jnp.swapaxes(state_conv, 1, 2)
    st_pool = jnp.swapaxes(state_pool, 1, 2)

    k_p, v_p, conv_p, pool_p, k_s, v_s, conv_s, pool_s = ([] for _ in range(8))
    g_pre4 = g_pre.reshape(depth, N_SUB, 1, d)
    g_post4 = g_post.reshape(depth, N_SUB, 1, d)
    pool_scale3 = pool_scale.reshape(pool_scale.shape[0], 1, d)
    for i in range(depth):
        gp = lambda j: _At(g_pre4, i, j)
        gq = lambda j: _At(g_post4, i, j)
        kind, slot = i % N_MIXERS, i // N_MIXERS
        ffn_w = lambda j: (_At(wg, i, j), _At(wu, i, j), _At(wd, i, j))

        xp = _ffn(xp, mod_p, i, 0, gp(0), gq(0), *ffn_w(0))
        xs = _ffn(xs, mod_all, i, 0, gp(0), gq(0), *ffn_w(0))

        if kind == 0:
            q, k, v, kb, vb = _qkv(xp, mod_p, i, gp(1), _At(wqkv, slot), BF16)
            o = _attn_prompt(q, kb, vb, sb_bias[slot])
            xp = _out_proj(xp, o, mod_p, i, gq(1), _At(wo, slot))
            k_p.append(k.reshape(bsz, seq, N_HEADS, head_dim))
            v_p.append(v.reshape(bsz, seq, N_HEADS, head_dim))

            q, k, v, kb, vb = _qkv(xs, mod_all, i, gp(1), _At(wqkv, slot), F32)
            to_seq = lambda a: jnp.swapaxes(a, 0, 1)
            o = _attn_sample(q, k, v, cache_k4, cache_v4, page_table, sb_bias[slot], slot)
            xs = _out_proj(xs, to_seq(o), mod_all, i, gq(1), _At(wo, slot))
            k_s.append(to_seq(k).reshape(db, dec_seq, N_HEADS, head_dim))
            v_s.append(to_seq(v).reshape(db, dec_seq, N_HEADS, head_dim))
        elif kind == 1:
            conv_ws = (_At(wbcx, slot), _At(conv_w, slot), _At(wco, slot))
            xp, tail = _conv_prompt(xp, mod_p, i, gp(1), gq(1), *conv_ws)
            conv_p.append(tail[:, SUBLANES - (CONV_W - 1):])
            xs, ns = _conv_sample(xs, st_conv[slot], mod_all, i, gp(1), gq(1), *conv_ws)
            conv_s.append(jnp.swapaxes(ns, 0, 1))
        else:
            pool_ws = (_At(wpl, slot), _At(pool_scale3, slot))
            xp, tail = _pool_prompt(xp, mod_p, i, gp(1), gq(1), *pool_ws)
            pool_p.append(tail[:, 2 * SUBLANES - POOL_HIST:])
            xs, ns = _pool_sample(xs, st_pool[slot], mod_all, i, gp(1), gq(1), *pool_ws, past_len)
            pool_s.append(jnp.swapaxes(ns, 0, 1))

        xp = _ffn(xp, mod_p, i, 2, gp(2), gq(2), *ffn_w(1))
        xs = _ffn(xs, mod_all, i, 2, gp(2), gq(2), *ffn_w(1))

    return (xp, jnp.swapaxes(xs, 0, 1), jnp.stack(k_p), jnp.stack(v_p), jnp.stack(conv_p), jnp.stack(pool_p),
            jnp.stack(k_s), jnp.stack(v_s), jnp.stack(conv_s), jnp.stack(pool_s))
```

```python
import functools

import jax
import jax.numpy as jnp
from jax import lax
from jax.experimental import pallas as pl
from jax.experimental.pallas import tpu as pltpu

F32 = jnp.float32
BF16 = jnp.bfloat16

NEG_LOG2E = -1.4426950408889634
MASKED_LOGIT = -1e30
RMS_EPS = 1e-6
FFN_HALF = 0.5
N_SUB = 3
N_MOD = 3
N_MIXERS = 3
N_HEADS = 16
CONV_W = 3
POOL_WINDOWS = (2, 4, 8, 16)
POOL_HIST = max(POOL_WINDOWS) - 1
PAGE_SIZE = 128

SUBLANES = 8
LANES = 128
ROW_TILE = 512
ATTN_TILE = 512
ATTN_SUB = 256
ATTN_PAIRS = 2
SAMPLE_PAGES_PER_STEP = 16
VMEM_LIMIT = 56 * 1024 * 1024


def _params(*sem):
    return pltpu.CompilerParams(dimension_semantics=sem, vmem_limit_bytes=VMEM_LIMIT)


def _const_spec(shape):
    zeros = (0,) * len(shape)
    return pl.BlockSpec(shape, lambda *_: zeros)


def _rms(x, g):
    return x * lax.rsqrt(jnp.mean(x * x, axis=-1, keepdims=True) + RMS_EPS) * g


def _mod(mod_ref, j, m, per_row):
    k = j * N_MOD + m
    return mod_ref[k] if per_row else mod_ref[k:k + 1, :]


def _pre(x, mod_ref, g, j, per_row):
    return _rms(x, g) * (1.0 + _mod(mod_ref, j, 1, per_row)) + _mod(mod_ref, j, 0, per_row)


def _post(x, o, mod_ref, g, j, weight, per_row):
    return x + (weight * _mod(mod_ref, j, 2, per_row)) * _rms(o, g)


def _flat(x):
    return x.reshape(x.shape[0] * x.shape[1], x.shape[2]) if x.ndim == 3 else x


def _dot(a, b):
    return jnp.dot(a, b, preferred_element_type=F32)


def _softplus(z):
    return jnp.maximum(z, 0.0) + jnp.log(1.0 + jnp.exp2(jnp.abs(z) * NEG_LOG2E))


def _ada_kernel(c_ref, w_ref, b_ref, o_ref):
    c = c_ref[...]
    c_act = (c * jax.nn.sigmoid(c)).astype(BF16)
    o_ref[...] = _dot(c_act, w_ref[...].astype(BF16)) + b_ref[...]


def _ada(c_all, w_ada, b_ada):
    depth, d, n = w_ada.shape
    n_comp = n // d
    rows = c_all.shape[0]
    return pl.pallas_call(
        _ada_kernel,
        grid=(depth, n_comp),
        in_specs=[
            _const_spec((rows, d)),
            pl.BlockSpec((None, d, d), lambda l, c: (l, 0, c)),
            pl.BlockSpec((None, None, 1, d), lambda l, c: (l, c, 0, 0)),
        ],
        out_specs=pl.BlockSpec((None, None, rows, d), lambda l, c: (l, c, 0, 0)),
        out_shape=jax.ShapeDtypeStruct((depth, n_comp, rows, d), F32),
        compiler_params=_params("arbitrary", "arbitrary"),
        name="ada_mod",
    )(c_all, w_ada, b_ada.reshape(depth, n_comp, 1, d))


class _Rows:
    def __init__(self, x, mod, layer):
        self.per_row = mod.shape[1] == N_SUB * N_MOD and mod.shape[2] != N_SUB * N_MOD
        d = x.shape[-1]
        if self.per_row:
            s, db, _ = x.shape
            self.grid = (1, 1)
            self.x_spec = pl.BlockSpec((s, db, d), lambda b, t: (0, 0, 0))
            self.mod_spec = pl.BlockSpec((None, N_SUB * N_MOD, db, d), lambda b, t: (layer, 0, 0, 0))
            self.tm = s * db
        else:
            bsz, t, _ = x.shape
            self.tm = min(ROW_TILE, t)
            self.grid = (bsz, t // self.tm)
            self.x_spec = pl.BlockSpec((None, self.tm, d), lambda b, t: (b, t, 0))
            self.mod_spec = pl.BlockSpec((None, None, N_SUB * N_MOD, d), lambda b, t: (layer, b, 0, 0))

    def like_x(self, width):
        shape = self.x_spec.block_shape[:-1] + (width,)
        return pl.BlockSpec(shape, self.x_spec.index_map)

    def tail_spec(self, rows, d):
        return pl.BlockSpec((None, rows, d), lambda b, t: (b, 0, 0))


class _At:
    def __init__(self, array, *lead):
        self.array, self.lead = array, lead


def _w_spec(w):
    shape = w.array.shape
    block = (None,) * len(w.lead) + shape[len(w.lead):]
    index = tuple(w.lead) + (0,) * (len(shape) - len(w.lead))
    return pl.BlockSpec(block, lambda *_: index, pipeline_mode=pl.Buffered(1))


def _ffn_kernel(x_ref, mod_ref, gpre_ref, gpost_ref, wg_ref, wu_ref, wd_ref, o_ref, *, j, per_row):
    x = x_ref[...]
    h = _flat(_pre(x, mod_ref, gpre_ref[...], j, per_row)).astype(BF16)
    g = _dot(h, wg_ref[...])
    u = _dot(h, wu_ref[...])
    a = ((g * jax.nn.sigmoid(g)) * u).astype(BF16)
    y = _dot(a, wd_ref[...]).reshape(x.shape)
    o_ref[...] = _post(x, y, mod_ref, gpost_ref[...], j, FFN_HALF, per_row)


def _ffn(x, mod, layer, j, gpre, gpost, wg, wu, wd):
    r = _Rows(x, mod, layer)
    d = x.shape[-1]
    return pl.pallas_call(
        functools.partial(_ffn_kernel, j=j, per_row=r.per_row),
        grid=r.grid,
        in_specs=[r.x_spec, r.mod_spec, _w_spec(gpre), _w_spec(gpost), _w_spec(wg), _w_spec(wu), _w_spec(wd)],
        out_specs=r.x_spec,
        out_shape=jax.ShapeDtypeStruct(x.shape, F32),
        compiler_params=_params("arbitrary", "arbitrary"),
        name="ffn",
    )(x, mod, gpre.array, gpost.array, wg.array, wu.array, wd.array)


def _qkv_kernel(x_ref, mod_ref, gpre_ref, w_ref, *refs, per_row, q_scale, n_prev):
    q_ref, k_ref, v_ref, *bf16_refs = refs[n_prev:]
    x = x_ref[...]
    d = x.shape[-1]
    h = _flat(_pre(x, mod_ref, gpre_ref[...], 1, per_row)).astype(BF16)
    qkv = _dot(h, w_ref[...])
    k = qkv[:, d:2 * d]
    v = qkv[:, 2 * d:]
    q_ref[...] = (qkv[:, :d] * q_scale).reshape(q_ref.shape).astype(q_ref.dtype)
    if bf16_refs:
        k_ref[...] = k.T
        v_ref[...] = v.T
        bf16_refs[0][...] = k.astype(BF16)
        bf16_refs[1][...] = v.astype(BF16)
    else:
        k_ref[...] = k.reshape(k_ref.shape)
        v_ref[...] = v.reshape(v_ref.shape)


def _qkv_sample(x, mod, layer, gpre, w):
    r = _Rows(x, mod, layer)
    d = x.shape[-1]
    return pl.pallas_call(
        functools.partial(_qkv_kernel, per_row=True, q_scale=(d // N_HEADS) ** -0.5, n_prev=0),
        grid=r.grid,
        in_specs=[r.x_spec, r.mod_spec, _w_spec(gpre), _w_spec(w)],
        out_specs=[r.x_spec] * 3,
        out_shape=[jax.ShapeDtypeStruct(x.shape, F32)] * 3,
        compiler_params=_params("arbitrary", "arbitrary"),
        name="qkv_proj_sample",
    )(x, mod, gpre.array, w.array)


def _qkv_prompt(x, mod, layer, gpre, w, slot, n_slots, prev):
    r = _Rows(x, mod, layer)
    bsz, t, d = x.shape
    n_prev = len(prev)
    stacked = jax.ShapeDtypeStruct((n_slots, bsz, d, t), F32)
    t_spec = pl.BlockSpec((None, None, d, r.tm), lambda b, i: (slot, b, 0, i))
    q, kt, vt, kb, vb = pl.pallas_call(
        functools.partial(_qkv_kernel, per_row=False, q_scale=(d // N_HEADS) ** -0.5, n_prev=n_prev),
        grid=r.grid,
        in_specs=[r.x_spec, r.mod_spec, _w_spec(gpre), _w_spec(w)] + [pl.BlockSpec(memory_space=pl.ANY)] * n_prev,
        out_specs=[r.x_spec, t_spec, t_spec, r.x_spec, r.x_spec],
        out_shape=[jax.ShapeDtypeStruct(x.shape, BF16), stacked, stacked,
                   jax.ShapeDtypeStruct(x.shape, BF16), jax.ShapeDtypeStruct(x.shape, BF16)],
        input_output_aliases={4 + n: 1 + n for n in range(n_prev)},
        compiler_params=_params("arbitrary", "arbitrary"),
        name="qkv_proj_prompt",
    )(x, mod, gpre.array, w.array, *prev)
    return q, kb, vb, (kt, vt)


def _out_proj_kernel(x_ref, o_ref, mod_ref, gpost_ref, w_ref, y_ref, *, per_row):
    x = x_ref[...]
    y = _dot(_flat(o_ref[...]).astype(BF16), w_ref[...]).reshape(x.shape)
    y_ref[...] = _post(x, y, mod_ref, gpost_ref[...], 1, 1.0, per_row)


def _out_proj(x, o, mod, layer, gpost, w):
    r = _Rows(x, mod, layer)
    return pl.pallas_call(
        functools.partial(_out_proj_kernel, per_row=r.per_row),
        grid=r.grid,
        in_specs=[r.x_spec, r.x_spec, r.mod_spec, _w_spec(gpost), _w_spec(w)],
        out_specs=r.x_spec,
        out_shape=jax.ShapeDtypeStruct(x.shape, F32),
        compiler_params=_params("arbitrary", "arbitrary"),
        name="out_proj",
    )(x, o, mod, gpost.array, w.array)


def _suffix_ones(n):
    j = lax.broadcasted_iota(jnp.int32, (n, n), 0)
    s = lax.broadcasted_iota(jnp.int32, (n, n), 1)
    return (j >= s).astype(BF16)


def _dot_nt(a, b):
    return lax.dot_general(a, b, (((1,), (1,)), ((), ())), preferred_element_type=F32)


def _sb_logits(z, uu, mask):
    sp = _softplus(z)
    if mask is not None:
        sp = jnp.where(mask, sp, 0.0)
        z = jnp.where(mask, z, MASKED_LOGIT)
    sub = uu.shape[1]
    parts, totals = [], []
    for s in range(z.shape[1] // sub):
        cols = slice(s * sub, (s + 1) * sub)
        suffix = _dot(sp[:, cols].astype(BF16), uu)
        parts.append(z[:, cols] - suffix)
        totals.append(jnp.broadcast_to(suffix[:, 0:1], (z.shape[0], LANES)))
    return (parts[0] if len(parts) == 1 else jnp.concatenate(parts, axis=1)), totals


def _sb_weights(t, totals, run_ref):
    run = run_ref[...]
    sub = t.shape[1] // len(totals)
    args = [None] * len(totals)
    for s in reversed(range(len(totals))):
        args[s] = t[:, s * sub:(s + 1) * sub] - jnp.concatenate([run] * (sub // LANES), axis=1)
        run = run + totals[s]
    run_ref[...] = run
    return jnp.exp(args[0] if len(args) == 1 else jnp.concatenate(args, axis=1)).astype(BF16)


def _attn_prompt_kernel(bias_ref, q_ref, k_ref, v_ref, uu_ref, o_ref, run_ref, acc_ref, *, head_dim):
    group = pl.program_id(1)
    i = pl.program_id(2)
    tq = q_ref.shape[0]
    pair = 2 * head_dim
    n_pairs = q_ref.shape[1] // pair
    lane = lax.broadcasted_iota(jnp.int32, (tq, pair), 1)
    row = lax.broadcasted_iota(jnp.int32, (2 * tq, tq), 0)
    col = lax.broadcasted_iota(jnp.int32, (2 * tq, tq), 1)
    mask = col < jnp.where(row < tq, row, row - tq)
    uu = uu_ref[...]
    run_ref[...] = jnp.zeros_like(run_ref)
    acc_ref[...] = jnp.zeros_like(acc_ref)

    def keys(j):
        return pl.ds(pl.multiple_of(j * tq, tq), tq)

    def stacked_q(p):
        q = q_ref[:, p * pair:(p + 1) * pair]
        zero = jnp.zeros_like(q)
        return jnp.concatenate([jnp.where(lane < head_dim, q, zero), jnp.where(lane >= head_dim, q, zero)], axis=0)

    qqs = [stacked_q(p) for p in range(n_pairs)]

    def block(j, mask):
        for p in range(n_pairs):
            lanes = slice(p * pair, (p + 1) * pair)
            z = _dot_nt(qqs[p], k_ref[keys(j), lanes])
            head = 2 * (group * n_pairs + p)
            z = jnp.concatenate([z[:tq] + bias_ref[head], z[tq:] + bias_ref[head + 1]], axis=0)
            a = _sb_weights(*_sb_logits(z, uu, mask), run_ref.at[p])
            acc_ref[p] += _dot(a, v_ref[keys(j), lanes])

    block(i, mask)

    def body(n, carry):
        block(i - 1 - n, None)
        return carry

    lax.fori_loop(0, i, body, 0)
    for p in range(n_pairs):
        acc = acc_ref[p]
        o_ref[:, p * pair:(p + 1) * pair] = jnp.where(lane < head_dim, acc[:tq], acc[tq:]).astype(o_ref.dtype)


def _attn_prompt(q, kb, vb, bias):
    bsz, t, d = q.shape
    tq = min(ATTN_TILE, t)
    sub = min(ATTN_SUB, tq)
    head_dim = d // N_HEADS
    pair = 2 * head_dim
    width = ATTN_PAIRS * pair
    blk = pl.BlockSpec((None, tq, width), lambda b, g, i: (b, i, g))
    seq = pl.BlockSpec((None, t, width), lambda b, g, i: (b, 0, g))
    return pl.pallas_call(
        functools.partial(_attn_prompt_kernel, head_dim=head_dim),
        grid=(bsz, d // width, t // tq),
        in_specs=[pl.BlockSpec(memory_space=pltpu.SMEM), blk, seq, seq,
                  pl.BlockSpec((sub, sub), lambda b, g, i: (0, 0))],
        out_specs=blk,
        out_shape=jax.ShapeDtypeStruct(q.shape, BF16),
        scratch_shapes=[pltpu.VMEM((ATTN_PAIRS, 2 * tq, LANES), F32), pltpu.VMEM((ATTN_PAIRS, 2 * tq, pair), F32)],
        compiler_params=_params("arbitrary", "arbitrary", "arbitrary"),
        name="sb_attn_prompt",
    )(bias, q, kb, vb, _suffix_ones(sub))


def _attn_sample_kernel(pt_ref, brow_ref, q_ref, k_ref, v_ref, *rest, head_dim, pages_per_step):
    pg = pages_per_step
    kp_refs, vp_refs = rest[:pg], rest[pg:2 * pg]
    uu_ref, o_ref, qbd_ref, knew_ref, vnew_ref, run_ref, acc_ref = rest[2 * pg:]
    b = pl.program_id(0)
    p = pl.program_id(1)
    s, _, d = q_ref.shape
    rows = s * N_HEADS
    row = lax.broadcasted_iota(jnp.int32, (rows, d), 0)
    lane = lax.broadcasted_iota(jnp.int32, (rows, d), 1)
    own_head = (lane // head_dim) == (row % N_HEADS)
    bias = brow_ref[...]
    uu = uu_ref[...]

    @pl.when(p == 0)
    def _():
        rep = jnp.concatenate([jnp.broadcast_to(q_ref[t, pl.ds(b, 1), :], (N_HEADS, d)) for t in range(s)], axis=0)
        qbd_ref[...] = jnp.where(own_head, rep, 0.0).astype(BF16)
        run_ref[...] = jnp.zeros_like(run_ref)
        acc_ref[...] = jnp.zeros_like(acc_ref)
        pad = 2 * SUBLANES
        r16 = lax.broadcasted_iota(jnp.int32, (pad, d), 0)
        kn = jnp.zeros((pad, d), F32)
        vn = jnp.zeros((pad, d), F32)
        for t in range(s):
            kn = jnp.where(r16 == t, jnp.broadcast_to(k_ref[t, pl.ds(b, 1), :], (pad, d)), kn)
            vn = jnp.where(r16 == t, jnp.broadcast_to(v_ref[t, pl.ds(b, 1), :], (pad, d)), vn)
        knew_ref[...] = jnp.zeros_like(knew_ref)
        vnew_ref[...] = jnp.zeros_like(vnew_ref)
        knew_ref[0:pad, :] = kn.astype(BF16)
        vnew_ref[0:pad, :] = vn.astype(BF16)
        r2 = lax.broadcasted_iota(jnp.int32, (rows, PAGE_SIZE), 0)
        c2 = lax.broadcasted_iota(jnp.int32, (rows, PAGE_SIZE), 1)
        new = _sb_logits(_dot_nt(qbd_ref[...], knew_ref[...]) + bias, uu, c2 < r2 // N_HEADS)
        acc_ref[...] += _dot(_sb_weights(*new, run_ref), vnew_ref[...])

    qbd = qbd_ref[...]
    z = jnp.concatenate([_dot(qbd, kp_refs[pg - 1 - c][...].astype(BF16)) + bias for c in range(pg)], axis=1)
    a = _sb_weights(*_sb_logits(z, uu, None), run_ref)
    acc = acc_ref[...]
    for c in range(pg):
        acc = acc + _dot_nt(a[:, c * PAGE_SIZE:(c + 1) * PAGE_SIZE], vp_refs[pg - 1 - c][...].astype(BF16))
    acc_ref[...] = acc

    @pl.when(p == pl.num_programs(1) - 1)
    def _():
        own = jnp.where(own_head, acc_ref[...], 0.0)
        o_ref[...] = jnp.sum(own.reshape(s, N_HEADS, d), axis=1)


def _attn_sample(q, k_new, v_new, cache_k, cache_v, page_table, bias, slot):
    s, db, d = q.shape
    n_pages = page_table.shape[1]
    head_dim = d // N_HEADS
    rows = s * N_HEADS
    pg = max(g for g in range(1, SAMPLE_PAGES_PER_STEP + 1) if n_pages % g == 0)
    bias_rows = jnp.broadcast_to(jnp.tile(bias, s)[:, None], (rows, PAGE_SIZE)).astype(F32)

    def page_spec(g):
        def page_map(b, p, pt):
            return (slot, pt[b * n_pages + n_pages - 1 - (p * pg + g)], 0, 0)
        return pl.BlockSpec((None, None, d, PAGE_SIZE), page_map)

    const = lambda shape: pl.BlockSpec(shape, lambda b, p, pt: (0,) * len(shape), pipeline_mode=pl.Buffered(1))
    grid_spec = pltpu.PrefetchScalarGridSpec(
        num_scalar_prefetch=1,
        grid=(db, n_pages // pg),
        in_specs=[const((rows, PAGE_SIZE)), const(q.shape), const(q.shape), const(q.shape)]
        + [page_spec(g) for g in range(pg)] * 2
        + [const((PAGE_SIZE, PAGE_SIZE))],
        out_specs=pl.BlockSpec((None, s, d), lambda b, p, pt: (b, 0, 0)),
        scratch_shapes=[
            pltpu.VMEM((rows, d), BF16),
            pltpu.VMEM((PAGE_SIZE, d), BF16),
            pltpu.VMEM((PAGE_SIZE, d), BF16),
            pltpu.VMEM((rows, PAGE_SIZE), F32),
            pltpu.VMEM((rows, d), F32),
        ],
    )
    return pl.pallas_call(
        functools.partial(_attn_sample_kernel, head_dim=head_dim, pages_per_step=pg),
        grid_spec=grid_spec,
        out_shape=jax.ShapeDtypeStruct((db, s, d), F32),
        compiler_params=_params("arbitrary", "arbitrary"),
        name="sb_attn_sample",
    )(page_table.reshape(-1), bias_rows, q, k_new, v_new, *([cache_k] * pg), *([cache_v] * pg),
      _suffix_ones(PAGE_SIZE))


def _shift_rows(u, prev, s):
    if s == SUBLANES:
        return jnp.concatenate([prev, u[:-SUBLANES]], axis=0)
    ru = pltpu.roll(u, s, axis=0)
    rp = pltpu.roll(prev, s, axis=0)
    r8 = lax.broadcasted_iota(jnp.int32, prev.shape, 0)
    return jnp.concatenate([jnp.where(r8 < s, rp, ru[:SUBLANES]), ru[SUBLANES:]], axis=0)


def _conv_prompt_kernel(x_ref, mod_ref, gpre_ref, gpost_ref, wbcx_ref, cw_ref, wout_ref, y_ref, tail_ref, carry_ref):
    @pl.when(pl.program_id(1) == 0)
    def _():
        carry_ref[...] = jnp.zeros_like(carry_ref)

    x = x_ref[...]
    d = x.shape[-1]
    h = _pre(x, mod_ref, gpre_ref[...], 1, False).astype(BF16)
    bcx = _dot(h, wbcx_ref[...])
    u = bcx[:, d:2 * d] * bcx[:, 2 * d:]
    prev = carry_ref[...]
    y = cw_ref[0:1, :] * _shift_rows(u, prev, 2) + cw_ref[1:2, :] * _shift_rows(u, prev, 1) + cw_ref[2:3, :] * u
    last = u[u.shape[0] - SUBLANES:, :]
    carry_ref[...] = last
    tail_ref[...] = last
    o = _dot((bcx[:, :d] * y).astype(BF16), wout_ref[...])
    y_ref[...] = _post(x, o, mod_ref, gpost_ref[...], 1, 1.0, False)


def _conv_prompt(x, mod, layer, gpre, gpost, wbcx, cw, wout):
    r = _Rows(x, mod, layer)
    bsz, _, d = x.shape
    return pl.pallas_call(
        _conv_prompt_kernel,
        grid=r.grid,
        in_specs=[r.x_spec, r.mod_spec, _w_spec(gpre), _w_spec(gpost), _w_spec(wbcx), _w_spec(cw), _w_spec(wout)],
        out_specs=[r.x_spec, r.tail_spec(SUBLANES, d)],
        out_shape=[jax.ShapeDtypeStruct(x.shape, F32), jax.ShapeDtypeStruct((bsz, SUBLANES, d), F32)],
        scratch_shapes=[pltpu.VMEM((SUBLANES, d), F32)],
        compiler_params=_params("arbitrary", "arbitrary"),
        name="conv_prompt",
    )(x, mod, gpre.array, gpost.array, wbcx.array, cw.array, wout.array)


def _conv_sample_kernel(x_ref, st_ref, mod_ref, gpre_ref, gpost_ref, wbcx_ref, cw_ref, wout_ref, y_ref, ns_ref):
    x = x_ref[...]
    s, db, d = x.shape
    h = _flat(_pre(x, mod_ref, gpre_ref[...], 1, True)).astype(BF16)
    bcx = _dot(h, wbcx_ref[...])
    u = (bcx[:, d:2 * d] * bcx[:, 2 * d:]).reshape(s, db, d)
    ext = [st_ref[j] for j in range(CONV_W - 1)] + [u[t] for t in range(s)]
    y = jnp.stack([sum(cw_ref[j:j + 1, :] * ext[t + j] for j in range(CONV_W)) for t in range(s)], axis=0)
    for j in range(CONV_W - 1):
        ns_ref[j] = ext[s + j]
    o = _dot((bcx[:, :d] * _flat(y)).astype(BF16), wout_ref[...]).reshape(x.shape)
    y_ref[...] = _post(x, o, mod_ref, gpost_ref[...], 1, 1.0, True)


def _conv_sample(x, state, mod, layer, gpre, gpost, wbcx, cw, wout):
    r = _Rows(x, mod, layer)
    st_spec = pl.BlockSpec(state.shape, lambda b, t: (0, 0, 0))
    return pl.pallas_call(
        _conv_sample_kernel,
        grid=r.grid,
        in_specs=[r.x_spec, st_spec, r.mod_spec, _w_spec(gpre), _w_spec(gpost), _w_spec(wbcx), _w_spec(cw), _w_spec(wout)],
        out_specs=[r.x_spec, st_spec],
        out_shape=[jax.ShapeDtypeStruct(x.shape, F32), jax.ShapeDtypeStruct(state.shape, F32)],
        compiler_params=_params("arbitrary", "arbitrary"),
        name="conv_sample",
    )(x, state, mod, gpre.array, gpost.array, wbcx.array, cw.array, wout.array)


def _pool_mix(diff, wp_ref, scale):
    g = diff.shape[-1] // len(POOL_WINDOWS)
    parts = [_dot(diff[:, i * g:(i + 1) * g].astype(BF16), wp_ref[i]) for i in range(len(POOL_WINDOWS))]
    return jnp.concatenate(parts, axis=-1) * scale


def _pool_prompt_kernel(x_ref, mod_ref, gpre_ref, gpost_ref, wp_ref, ps_ref, y_ref, tail_ref, carry_ref):
    it = pl.program_id(1)

    @pl.when(it == 0)
    def _():
        carry_ref[...] = jnp.zeros_like(carry_ref)

    x = x_ref[...]
    tm, d = x.shape
    g = d // len(POOL_WINDOWS)
    h = _pre(x, mod_ref, gpre_ref[...], 1, False)
    sums = [h]
    for k in range(len(POOL_WINDOWS)):
        cur = sums[-1]
        sums.append(cur + _shift_rows(cur, carry_ref[k], 2 ** k))
        carry_ref[k] = cur[tm - SUBLANES:, :]
    assert POOL_WINDOWS == tuple(2 ** (k + 1) for k in range(len(POOL_WINDOWS)))
    total = jnp.concatenate([sums[k + 1][:, k * g:(k + 1) * g] for k in range(len(POOL_WINDOWS))], axis=-1)
    lane = lax.broadcasted_iota(jnp.int32, (tm, d), 1)
    pos = lax.broadcasted_iota(jnp.int32, (tm, d), 0) + it * tm
    window = jnp.left_shift(2, lane // g)
    count = jnp.minimum(pos + 1, window).astype(F32)
    diff = total / count - h
    tail_ref[...] = h[tm - 2 * SUBLANES:, :]
    y_ref[...] = _post(x, _pool_mix(diff, wp_ref, ps_ref[...]), mod_ref, gpost_ref[...], 1, 1.0, False)


def _pool_prompt(x, mod, layer, gpre, gpost, wp, ps):
    r = _Rows(x, mod, layer)
    bsz, _, d = x.shape
    return pl.pallas_call(
        _pool_prompt_kernel,
        grid=r.grid,
        in_specs=[r.x_spec, r.mod_spec, _w_spec(gpre), _w_spec(gpost), _w_spec(wp), _w_spec(ps)],
        out_specs=[r.x_spec, r.tail_spec(2 * SUBLANES, d)],
        out_shape=[jax.ShapeDtypeStruct(x.shape, F32), jax.ShapeDtypeStruct((bsz, 2 * SUBLANES, d), F32)],
        scratch_shapes=[pltpu.VMEM((len(POOL_WINDOWS), SUBLANES, d), F32)],
        compiler_params=_params("arbitrary", "arbitrary"),
        name="pool_prompt",
    )(x, mod, gpre.array, gpost.array, wp.array, ps.array)


def _pool_sample_kernel(x_ref, st_ref, mod_ref, gpre_ref, gpost_ref, wp_ref, ps_ref, y_ref, ns_ref, *, past_len):
    x = x_ref[...]
    s, db, d = x.shape
    g = d // len(POOL_WINDOWS)
    hist = st_ref.shape[0]
    h = _pre(x, mod_ref, gpre_ref[...], 1, True)
    ext = [st_ref[j] for j in range(hist)] + [h[t] for t in range(s)]
    means = []
    for t in range(s):
        parts = []
        for k, w in enumerate(POOL_WINDOWS):
            sl = slice(k * g, (k + 1) * g)
            tot = ext[hist + t][:, sl]
            for j in range(1, w):
                tot = tot + ext[hist + t - j][:, sl]
            parts.append(tot / float(min(past_len + t + 1, w)))
        means.append(jnp.concatenate(parts, axis=-1))
    diff = _flat(jnp.stack(means, axis=0) - h)
    for j in range(hist):
        ns_ref[j] = ext[s + j]
    mixed = _pool_mix(diff, wp_ref, ps_ref[...]).reshape(x.shape)
    y_ref[...] = _post(x, mixed, mod_ref, gpost_ref[...], 1, 1.0, True)


def _pool_sample(x, state, mod, layer, gpre, gpost, wp, ps, past_len):
    r = _Rows(x, mod, layer)
    st_spec = pl.BlockSpec(state.shape, lambda b, t: (0, 0, 0))
    return pl.pallas_call(
        functools.partial(_pool_sample_kernel, past_len=past_len),
        grid=r.grid,
        in_specs=[r.x_spec, st_spec, r.mod_spec, _w_spec(gpre), _w_spec(gpost), _w_spec(wp), _w_spec(ps)],
        out_specs=[r.x_spec, st_spec],
        out_shape=[jax.ShapeDtypeStruct(x.shape, F32), jax.ShapeDtypeStruct(state.shape, F32)],
        compiler_params=_params("arbitrary", "arbitrary"),
        name="pool_sample",
    )(x, state, mod, gpre.array, gpost.array, wp.array, ps.array)


def kernel(x_prompt, x_sample, cache_k, cache_v, state_conv, state_pool, page_table, c_prompt, c_sample,
           w_ada, b_ada, g_pre, g_post, w_ffn_gate, w_ffn_up, w_ffn_down, w_qkv, w_o, sb_bias, w_bcx,
           conv_w, w_conv_out, w_pool, pool_scale):
    bsz, seq, d = x_prompt.shape
    db, dec_seq, _ = x_sample.shape
    depth = w_ada.shape[0]
    head_dim = d // N_HEADS
    n_pages = page_table.shape[1]
    past_len = n_pages * PAGE_SIZE
    assert dec_seq <= 2 * SUBLANES and db % SUBLANES == 0

    mod_all = _ada(jnp.concatenate([c_sample, c_prompt], axis=0), w_ada, b_ada)
    mod_p = jnp.swapaxes(mod_all[:, :, db:], 1, 2)

    wg, wu, wd = (w.astype(BF16) for w in (w_ffn_gate, w_ffn_up, w_ffn_down))
    wqkv, wo, wbcx, wco, wpl = (w.astype(BF16) for w in (w_qkv, w_o, w_bcx, w_conv_out, w_pool))
    pages = lambda c: jnp.transpose(c, (0, 1, 3, 4, 2)).reshape(c.shape[0], c.shape[1], d, PAGE_SIZE)
    cache_k4, cache_v4 = pages(cache_k), pages(cache_v)

    xp = x_prompt
    xs = jnp.swapaxes(x_sample, 0, 1)
    st_conv = jnp.swapaxes(state_conv, 1, 2)
    st_pool = jnp.swapaxes(state_pool, 1, 2)

    conv_p, pool_p, k_s, v_s, conv_s, pool_s = ([] for _ in range(6))
    kv_t = ()
    g_pre4 = g_pre.reshape(depth, N_SUB, 1, d)
    g_post4 = g_post.reshape(depth, N_SUB, 1, d)
    pool_scale3 = pool_scale.reshape(pool_scale.shape[0], 1, d)
    for i in range(depth):
        gp = lambda j: _At(g_pre4, i, j)
        gq = lambda j: _At(g_post4, i, j)
        kind, slot = i % N_MIXERS, i // N_MIXERS
        ffn_w = lambda j: (_At(wg, i, j), _At(wu, i, j), _At(wd, i, j))

        xp = _ffn(xp, mod_p, i, 0, gp(0), gq(0), *ffn_w(0))
        xs = _ffn(xs, mod_all, i, 0, gp(0), gq(0), *ffn_w(0))

        if kind == 0:
            q, kb, vb, kv_t = _qkv_prompt(xp, mod_p, i, gp(1), _At(wqkv, slot), slot, w_qkv.shape[0], kv_t)
            o = _attn_prompt(q, kb, vb, sb_bias[slot])
            xp = _out_proj(xp, o, mod_p, i, gq(1), _At(wo, slot))

            q, k, v = _qkv_sample(xs, mod_all, i, gp(1), _At(wqkv, slot))
            to_seq = lambda a: jnp.swapaxes(a, 0, 1)
            o = _attn_sample(q, k, v, cache_k4, cache_v4, page_table, sb_bias[slot], slot)
            xs = _out_proj(xs, to_seq(o), mod_all, i, gq(1), _At(wo, slot))
            k_s.append(to_seq(k).reshape(db, dec_seq, N_HEADS, head_dim))
            v_s.append(to_seq(v).reshape(db, dec_seq, N_HEADS, head_dim))
        elif kind == 1:
            conv_ws = (_At(wbcx, slot), _At(conv_w, slot), _At(wco, slot))
            xp, tail = _conv_prompt(xp, mod_p, i, gp(1), gq(1), *conv_ws)
            conv_p.append(tail[:, SUBLANES - (CONV_W - 1):])
            xs, ns = _conv_sample(xs, st_conv[slot], mod_all, i, gp(1), gq(1), *conv_ws)
            conv_s.append(jnp.swapaxes(ns, 0, 1))
        else:
            pool_ws = (_At(wpl, slot), _At(pool_scale3, slot))
            xp, tail = _pool_prompt(xp, mod_p, i, gp(1), gq(1), *pool_ws)
            pool_p.append(tail[:, 2 * SUBLANES - POOL_HIST:])
            xs, ns = _pool_sample(xs, st_pool[slot], mod_all, i, gp(1), gq(1), *pool_ws, past_len)
            pool_s.append(jnp.swapaxes(ns, 0, 1))

        xp = _ffn(xp, mod_p, i, 2, gp(2), gq(2), *ffn_w(1))
        xs = _ffn(xs, mod_all, i, 2, gp(2), gq(2), *ffn_w(1))

    heads = lambda a: jnp.transpose(a.reshape(a.shape[0], bsz, N_HEADS, head_dim, seq), (0, 1, 4, 2, 3))
    return (xp, jnp.swapaxes(xs, 0, 1), heads(kv_t[0]), heads(kv_t[1]), jnp.stack(conv_p), jnp.stack(pool_p),
            jnp.stack(k_s), jnp.stack(v_s), jnp.stack(conv_s), jnp.stack(pool_s))
```

```python
import functools

import jax
import jax.numpy as jnp
from jax import lax
from jax.experimental import pallas as pl
from jax.experimental.pallas import tpu as pltpu

F32 = jnp.float32
BF16 = jnp.bfloat16

NEG_LOG2E = -1.4426950408889634
MASKED_LOGIT = -1e30
RMS_EPS = 1e-6
FFN_HALF = 0.5
N_SUB = 3
N_MOD = 3
N_MIXERS = 3
N_HEADS = 16
CONV_W = 3
POOL_WINDOWS = (2, 4, 8, 16)
POOL_HIST = max(POOL_WINDOWS) - 1
PAGE_SIZE = 128

SUBLANES = 8
LANES = 128
ROW_TILE = 512
ATTN_TILE = 512
ATTN_SUB = 256
ATTN_PAIRS = 4
SAMPLE_PAGES_PER_STEP = 16
VMEM_LIMIT = 56 * 1024 * 1024


def _params(*sem):
    return pltpu.CompilerParams(dimension_semantics=sem, vmem_limit_bytes=VMEM_LIMIT)


def _const_spec(shape):
    zeros = (0,) * len(shape)
    return pl.BlockSpec(shape, lambda *_: zeros)


def _rms(x, g):
    return x * lax.rsqrt(jnp.mean(x * x, axis=-1, keepdims=True) + RMS_EPS) * g


def _mod(mod_ref, j, m, per_row):
    k = j * N_MOD + m
    return mod_ref[k] if per_row else mod_ref[k:k + 1, :]


def _pre(x, mod_ref, g, j, per_row):
    return _rms(x, g) * (1.0 + _mod(mod_ref, j, 1, per_row)) + _mod(mod_ref, j, 0, per_row)


def _post(x, o, mod_ref, g, j, weight, per_row):
    return x + (weight * _mod(mod_ref, j, 2, per_row)) * _rms(o, g)


def _flat(x):
    return x.reshape(x.shape[0] * x.shape[1], x.shape[2]) if x.ndim == 3 else x


def _dot(a, b):
    return jnp.dot(a, b, preferred_element_type=F32)


def _softplus(z):
    return jnp.maximum(z, 0.0) + jnp.log(1.0 + jnp.exp2(jnp.abs(z) * NEG_LOG2E))


def _ada_kernel(c_ref, w_ref, b_ref, o_ref):
    c = c_ref[...]
    c_act = (c * jax.nn.sigmoid(c)).astype(BF16)
    o_ref[...] = _dot(c_act, w_ref[...].astype(BF16)) + b_ref[...]


def _ada(c_all, w_ada, b_ada):
    depth, d, n = w_ada.shape
    n_comp = n // d
    rows = c_all.shape[0]
    return pl.pallas_call(
        _ada_kernel,
        grid=(depth, n_comp),
        in_specs=[
            _const_spec((rows, d)),
            pl.BlockSpec((None, d, d), lambda l, c: (l, 0, c)),
            pl.BlockSpec((None, None, 1, d), lambda l, c: (l, c, 0, 0)),
        ],
        out_specs=pl.BlockSpec((None, None, rows, d), lambda l, c: (l, c, 0, 0)),
        out_shape=jax.ShapeDtypeStruct((depth, n_comp, rows, d), F32),
        compiler_params=_params("arbitrary", "arbitrary"),
        name="ada_mod",
    )(c_all, w_ada, b_ada.reshape(depth, n_comp, 1, d))


class _Rows:
    def __init__(self, x, mod, layer):
        self.per_row = mod.shape[1] == N_SUB * N_MOD and mod.shape[2] != N_SUB * N_MOD
        d = x.shape[-1]
        if self.per_row:
            s, db, _ = x.shape
            self.grid = (1, 1)
            self.x_spec = pl.BlockSpec((s, db, d), lambda b, t: (0, 0, 0))
            self.mod_spec = pl.BlockSpec((None, N_SUB * N_MOD, db, d), lambda b, t: (layer, 0, 0, 0))
            self.tm = s * db
        else:
            bsz, t, _ = x.shape
            self.tm = min(ROW_TILE, t)
            self.grid = (bsz, t // self.tm)
            self.x_spec = pl.BlockSpec((None, self.tm, d), lambda b, t: (b, t, 0))
            self.mod_spec = pl.BlockSpec((None, None, N_SUB * N_MOD, d), lambda b, t: (layer, b, 0, 0))

    def like_x(self, width):
        shape = self.x_spec.block_shape[:-1] + (width,)
        return pl.BlockSpec(shape, self.x_spec.index_map)

    def tail_spec(self, rows, d):
        return pl.BlockSpec((None, rows, d), lambda b, t: (b, 0, 0))


class _At:
    def __init__(self, array, *lead):
        self.array, self.lead = array, lead


def _w_spec(w):
    shape = w.array.shape
    block = (None,) * len(w.lead) + shape[len(w.lead):]
    index = tuple(w.lead) + (0,) * (len(shape) - len(w.lead))
    return pl.BlockSpec(block, lambda *_: index, pipeline_mode=pl.Buffered(1))


def _ffn_apply(x, mod_ref, ffn_refs, j, per_row):
    gpre_ref, gpost_ref, wg_ref, wu_ref, wd_ref = ffn_refs
    h = _flat(_pre(x, mod_ref, gpre_ref[...], j, per_row)).astype(BF16)
    g = _dot(h, wg_ref[...])
    u = _dot(h, wu_ref[...])
    a = ((g * jax.nn.sigmoid(g)) * u).astype(BF16)
    y = _dot(a, wd_ref[...]).reshape(x.shape)
    return _post(x, y, mod_ref, gpost_ref[...], j, FFN_HALF, per_row)


def _ffn_kernel(x_ref, mod_ref, *refs, j, per_row):
    refs[-1][...] = _ffn_apply(x_ref[...], mod_ref, refs[:-1], j, per_row)


def _ffn(x, mod, layer, j, gpre, gpost, wg, wu, wd):
    r = _Rows(x, mod, layer)
    d = x.shape[-1]
    return pl.pallas_call(
        functools.partial(_ffn_kernel, j=j, per_row=r.per_row),
        grid=r.grid,
        in_specs=[r.x_spec, r.mod_spec, _w_spec(gpre), _w_spec(gpost), _w_spec(wg), _w_spec(wu), _w_spec(wd)],
        out_specs=r.x_spec,
        out_shape=jax.ShapeDtypeStruct(x.shape, F32),
        compiler_params=_params("arbitrary", "arbitrary"),
        name="ffn",
    )(x, mod, gpre.array, gpost.array, wg.array, wu.array, wd.array)


def _qkv_kernel(x_ref, mod_ref, gpre_ref, w_ref, *refs, per_row, q_scale, n_prev):
    q_ref, k_ref, v_ref, *bf16_refs = refs[n_prev:]
    x = x_ref[...]
    d = x.shape[-1]
    h = _flat(_pre(x, mod_ref, gpre_ref[...], 1, per_row)).astype(BF16)
    qkv = _dot(h, w_ref[...])
    k = qkv[:, d:2 * d]
    v = qkv[:, 2 * d:]
    q_ref[...] = (qkv[:, :d] * q_scale).reshape(q_ref.shape).astype(q_ref.dtype)
    if bf16_refs:
        k_ref[...] = k.T
        v_ref[...] = v.T
        bf16_refs[0][...] = k.astype(BF16)
        bf16_refs[1][...] = v.astype(BF16)
    else:
        k_ref[...] = k.reshape(k_ref.shape)
        v_ref[...] = v.reshape(v_ref.shape)


def _qkv_sample(x, mod, layer, gpre, w):
    r = _Rows(x, mod, layer)
    d = x.shape[-1]
    return pl.pallas_call(
        functools.partial(_qkv_kernel, per_row=True, q_scale=(d // N_HEADS) ** -0.5, n_prev=0),
        grid=r.grid,
        in_specs=[r.x_spec, r.mod_spec, _w_spec(gpre), _w_spec(w)],
        out_specs=[r.x_spec] * 3,
        out_shape=[jax.ShapeDtypeStruct(x.shape, F32)] * 3,
        compiler_params=_params("arbitrary", "arbitrary"),
        name="qkv_proj_sample",
    )(x, mod, gpre.array, w.array)


def _qkv_prompt(x, mod, layer, gpre, w, slot, n_slots, prev):
    r = _Rows(x, mod, layer)
    bsz, t, d = x.shape
    n_prev = len(prev)
    stacked = jax.ShapeDtypeStruct((n_slots, bsz, d, t), F32)
    t_spec = pl.BlockSpec((None, None, d, r.tm), lambda b, i: (slot, b, 0, i))
    q, kt, vt, kb, vb = pl.pallas_call(
        functools.partial(_qkv_kernel, per_row=False, q_scale=(d // N_HEADS) ** -0.5, n_prev=n_prev),
        grid=r.grid,
        in_specs=[r.x_spec, r.mod_spec, _w_spec(gpre), _w_spec(w)] + [pl.BlockSpec(memory_space=pl.ANY)] * n_prev,
        out_specs=[r.x_spec, t_spec, t_spec, r.x_spec, r.x_spec],
        out_shape=[jax.ShapeDtypeStruct(x.shape, BF16), stacked, stacked,
                   jax.ShapeDtypeStruct(x.shape, BF16), jax.ShapeDtypeStruct(x.shape, BF16)],
        input_output_aliases={4 + n: 1 + n for n in range(n_prev)},
        compiler_params=_params("arbitrary", "arbitrary"),
        name="qkv_proj_prompt",
    )(x, mod, gpre.array, w.array, *prev)
    return q, kb, vb, (kt, vt)


def _out_proj_kernel(x_ref, o_ref, mod_ref, gpost_ref, w_ref, *refs, per_row):
    x = x_ref[...]
    y = _dot(_flat(o_ref[...]).astype(BF16), w_ref[...]).reshape(x.shape)
    x = _post(x, y, mod_ref, gpost_ref[...], 1, 1.0, per_row)
    refs[-1][...] = _ffn_apply(x, mod_ref, refs[:-1], 2, per_row)


def _out_proj(x, o, mod, layer, gpost, w, *ffn_params):
    r = _Rows(x, mod, layer)
    return pl.pallas_call(
        functools.partial(_out_proj_kernel, per_row=r.per_row),
        grid=r.grid,
        in_specs=[r.x_spec, r.x_spec, r.mod_spec, _w_spec(gpost), _w_spec(w)] + [_w_spec(p) for p in ffn_params],
        out_specs=r.x_spec,
        out_shape=jax.ShapeDtypeStruct(x.shape, F32),
        compiler_params=_params("arbitrary", "arbitrary"),
        name="out_proj_ffn",
    )(x, o, mod, gpost.array, w.array, *(p.array for p in ffn_params))


def _suffix_ones(n):
    j = lax.broadcasted_iota(jnp.int32, (n, n), 0)
    s = lax.broadcasted_iota(jnp.int32, (n, n), 1)
    return (j >= s).astype(BF16)


def _dot_nt(a, b):
    return lax.dot_general(a, b, (((1,), (1,)), ((), ())), preferred_element_type=F32)


def _sb_logits(z, uu, mask):
    sp = _softplus(z)
    if mask is not None:
        sp = jnp.where(mask, sp, 0.0)
        z = jnp.where(mask, z, MASKED_LOGIT)
    sub = uu.shape[1]
    parts, totals = [], []
    for s in range(z.shape[1] // sub):
        cols = slice(s * sub, (s + 1) * sub)
        suffix = _dot(sp[:, cols].astype(BF16), uu)
        parts.append(z[:, cols] - suffix)
        totals.append(jnp.broadcast_to(suffix[:, 0:1], (z.shape[0], LANES)))
    return (parts[0] if len(parts) == 1 else jnp.concatenate(parts, axis=1)), totals


def _sb_weights(t, totals, run_ref):
    run = run_ref[...]
    sub = t.shape[1] // len(totals)
    args = [None] * len(totals)
    for s in reversed(range(len(totals))):
        args[s] = t[:, s * sub:(s + 1) * sub] - jnp.concatenate([run] * (sub // LANES), axis=1)
        run = run + totals[s]
    run_ref[...] = run
    return jnp.exp(args[0] if len(args) == 1 else jnp.concatenate(args, axis=1)).astype(BF16)


def _attn_prompt_kernel(bias_ref, q_ref, k_ref, v_ref, uu_ref, o_ref, run_ref, acc_ref, *, head_dim):
    group = pl.program_id(1)
    i = pl.program_id(2)
    tq = q_ref.shape[0]
    pair = 2 * head_dim
    n_pairs = q_ref.shape[1] // pair
    lane = lax.broadcasted_iota(jnp.int32, (tq, pair), 1)
    row = lax.broadcasted_iota(jnp.int32, (2 * tq, tq), 0)
    col = lax.broadcasted_iota(jnp.int32, (2 * tq, tq), 1)
    mask = col < jnp.where(row < tq, row, row - tq)
    uu = uu_ref[...]
    run_ref[...] = jnp.zeros_like(run_ref)
    acc_ref[...] = jnp.zeros_like(acc_ref)

    def keys(j):
        return pl.ds(pl.multiple_of(j * tq, tq), tq)

    def stacked_q(p):
        q = q_ref[:, p * pair:(p + 1) * pair]
        zero = jnp.zeros_like(q)
        return jnp.concatenate([jnp.where(lane < head_dim, q, zero), jnp.where(lane >= head_dim, q, zero)], axis=0)

    qqs = [stacked_q(p) for p in range(n_pairs)]

    def block(j, mask):
        for p in range(n_pairs):
            lanes = slice(p * pair, (p + 1) * pair)
            z = _dot_nt(qqs[p], k_ref[keys(j), lanes])
            head = 2 * (group * n_pairs + p)
            z = jnp.concatenate([z[:tq] + bias_ref[head], z[tq:] + bias_ref[head + 1]], axis=0)
            a = _sb_weights(*_sb_logits(z, uu, mask), run_ref.at[p])
            acc_ref[p] += _dot(a, v_ref[keys(j), lanes])

    block(i, mask)

    def body(n, carry):
        block(i - 1 - n, None)
        return carry

    lax.fori_loop(0, i, body, 0)
    for p in range(n_pairs):
        acc = acc_ref[p]
        o_ref[:, p * pair:(p + 1) * pair] = jnp.where(lane < head_dim, acc[:tq], acc[tq:]).astype(o_ref.dtype)


def _attn_prompt(q, kb, vb, bias):
    bsz, t, d = q.shape
    tq = min(ATTN_TILE, t)
    sub = min(ATTN_SUB, tq)
    head_dim = d // N_HEADS
    pair = 2 * head_dim
    width = ATTN_PAIRS * pair
    blk = pl.BlockSpec((None, tq, width), lambda b, g, i: (b, i, g))
    seq = pl.BlockSpec((None, t, width), lambda b, g, i: (b, 0, g))
    return pl.pallas_call(
        functools.partial(_attn_prompt_kernel, head_dim=head_dim),
        grid=(bsz, d // width, t // tq),
        in_specs=[pl.BlockSpec(memory_space=pltpu.SMEM), blk, seq, seq,
                  pl.BlockSpec((sub, sub), lambda b, g, i: (0, 0))],
        out_specs=blk,
        out_shape=jax.ShapeDtypeStruct(q.shape, BF16),
        scratch_shapes=[pltpu.VMEM((ATTN_PAIRS, 2 * tq, LANES), F32), pltpu.VMEM((ATTN_PAIRS, 2 * tq, pair), F32)],
        compiler_params=_params("arbitrary", "arbitrary", "arbitrary"),
        name="sb_attn_prompt",
    )(bias, q, kb, vb, _suffix_ones(sub))


def _attn_sample_kernel(pt_ref, brow_ref, q_ref, k_ref, v_ref, *rest, head_dim, pages_per_step):
    pg = pages_per_step
    kp_refs, vp_refs = rest[:pg], rest[pg:2 * pg]
    uu_ref, o_ref, qbd_ref, knew_ref, vnew_ref, run_ref, acc_ref = rest[2 * pg:]
    b = pl.program_id(0)
    p = pl.program_id(1)
    s, _, d = q_ref.shape
    rows = s * N_HEADS
    row = lax.broadcasted_iota(jnp.int32, (rows, d), 0)
    lane = lax.broadcasted_iota(jnp.int32, (rows, d), 1)
    own_head = (lane // head_dim) == (row % N_HEADS)
    bias = brow_ref[...]
    uu = uu_ref[...]

    @pl.when(p == 0)
    def _():
        rep = jnp.concatenate([jnp.broadcast_to(q_ref[t, pl.ds(b, 1), :], (N_HEADS, d)) for t in range(s)], axis=0)
        qbd_ref[...] = jnp.where(own_head, rep, 0.0).astype(BF16)
        run_ref[...] = jnp.zeros_like(run_ref)
        acc_ref[...] = jnp.zeros_like(acc_ref)
        pad = 2 * SUBLANES
        r16 = lax.broadcasted_iota(jnp.int32, (pad, d), 0)
        kn = jnp.zeros((pad, d), F32)
        vn = jnp.zeros((pad, d), F32)
        for t in range(s):
            kn = jnp.where(r16 == t, jnp.broadcast_to(k_ref[t, pl.ds(b, 1), :], (pad, d)), kn)
            vn = jnp.where(r16 == t, jnp.broadcast_to(v_ref[t, pl.ds(b, 1), :], (pad, d)), vn)
        knew_ref[...] = jnp.zeros_like(knew_ref)
        vnew_ref[...] = jnp.zeros_like(vnew_ref)
        knew_ref[0:pad, :] = kn.astype(BF16)
        vnew_ref[0:pad, :] = vn.astype(BF16)
        r2 = lax.broadcasted_iota(jnp.int32, (rows, PAGE_SIZE), 0)
        c2 = lax.broadcasted_iota(jnp.int32, (rows, PAGE_SIZE), 1)
        new = _sb_logits(_dot_nt(qbd_ref[...], knew_ref[...]) + bias, uu, c2 < r2 // N_HEADS)
        acc_ref[...] += _dot(_sb_weights(*new, run_ref), vnew_ref[...])

    qbd = qbd_ref[...]
    z = jnp.concatenate([_dot(qbd, kp_refs[pg - 1 - c][...].astype(BF16)) + bias for c in range(pg)], axis=1)
    a = _sb_weights(*_sb_logits(z, uu, None), run_ref)
    acc = acc_ref[...]
    for c in range(pg):
        acc = acc + _dot_nt(a[:, c * PAGE_SIZE:(c + 1) * PAGE_SIZE], vp_refs[pg - 1 - c][...].astype(BF16))
    acc_ref[...] = acc

    @pl.when(p == pl.num_programs(1) - 1)
    def _():
        own = jnp.where(own_head, acc_ref[...], 0.0)
        o_ref[...] = jnp.sum(own.reshape(s, N_HEADS, d), axis=1)


def _attn_sample(q, k_new, v_new, cache_k, cache_v, page_table, bias, slot):
    s, db, d = q.shape
    n_pages = page_table.shape[1]
    head_dim = d // N_HEADS
    rows = s * N_HEADS
    pg = max(g for g in range(1, SAMPLE_PAGES_PER_STEP + 1) if n_pages % g == 0)
    bias_rows = jnp.broadcast_to(jnp.tile(bias, s)[:, None], (rows, PAGE_SIZE)).astype(F32)

    def page_spec(g):
        def page_map(b, p, pt):
            return (slot, pt[b * n_pages + n_pages - 1 - (p * pg + g)], 0, 0)
        return pl.BlockSpec((None, None, d, PAGE_SIZE), page_map)

    const = lambda shape: pl.BlockSpec(shape, lambda b, p, pt: (0,) * len(shape), pipeline_mode=pl.Buffered(1))
    grid_spec = pltpu.PrefetchScalarGridSpec(
        num_scalar_prefetch=1,
        grid=(db, n_pages // pg),
        in_specs=[const((rows, PAGE_SIZE)), const(q.shape), const(q.shape), const(q.shape)]
        + [page_spec(g) for g in range(pg)] * 2
        + [const((PAGE_SIZE, PAGE_SIZE))],
        out_specs=pl.BlockSpec((None, s, d), lambda b, p, pt: (b, 0, 0)),
        scratch_shapes=[
            pltpu.VMEM((rows, d), BF16),
            pltpu.VMEM((PAGE_SIZE, d), BF16),
            pltpu.VMEM((PAGE_SIZE, d), BF16),
            pltpu.VMEM((rows, PAGE_SIZE), F32),
            pltpu.VMEM((rows, d), F32),
        ],
    )
    return pl.pallas_call(
        functools.partial(_attn_sample_kernel, head_dim=head_dim, pages_per_step=pg),
        grid_spec=grid_spec,
        out_shape=jax.ShapeDtypeStruct((db, s, d), F32),
        compiler_params=_params("arbitrary", "arbitrary"),
        name="sb_attn_sample",
    )(page_table.reshape(-1), bias_rows, q, k_new, v_new, *([cache_k] * pg), *([cache_v] * pg),
      _suffix_ones(PAGE_SIZE))


def _shift_rows(u, prev, s):
    if s == SUBLANES:
        return jnp.concatenate([prev, u[:-SUBLANES]], axis=0)
    ru = pltpu.roll(u, s, axis=0)
    rp = pltpu.roll(prev, s, axis=0)
    r8 = lax.broadcasted_iota(jnp.int32, prev.shape, 0)
    return jnp.concatenate([jnp.where(r8 < s, rp, ru[:SUBLANES]), ru[SUBLANES:]], axis=0)


def _conv_prompt_kernel(x_ref, mod_ref, gpre_ref, gpost_ref, wbcx_ref, cw_ref, wout_ref, y_ref, tail_ref, carry_ref):
    @pl.when(pl.program_id(1) == 0)
    def _():
        carry_ref[...] = jnp.zeros_like(carry_ref)

    x = x_ref[...]
    d = x.shape[-1]
    h = _pre(x, mod_ref, gpre_ref[...], 1, False).astype(BF16)
    bcx = _dot(h, wbcx_ref[...])
    u = bcx[:, d:2 * d] * bcx[:, 2 * d:]
    prev = carry_ref[...]
    y = cw_ref[0:1, :] * _shift_rows(u, prev, 2) + cw_ref[1:2, :] * _shift_rows(u, prev, 1) + cw_ref[2:3, :] * u
    last = u[u.shape[0] - SUBLANES:, :]
    carry_ref[...] = last
    tail_ref[...] = last
    o = _dot((bcx[:, :d] * y).astype(BF16), wout_ref[...])
    y_ref[...] = _post(x, o, mod_ref, gpost_ref[...], 1, 1.0, False)


def _conv_prompt(x, mod, layer, gpre, gpost, wbcx, cw, wout):
    r = _Rows(x, mod, layer)
    bsz, _, d = x.shape
    return pl.pallas_call(
        _conv_prompt_kernel,
        grid=r.grid,
        in_specs=[r.x_spec, r.mod_spec, _w_spec(gpre), _w_spec(gpost), _w_spec(wbcx), _w_spec(cw), _w_spec(wout)],
        out_specs=[r.x_spec, r.tail_spec(SUBLANES, d)],
        out_shape=[jax.ShapeDtypeStruct(x.shape, F32), jax.ShapeDtypeStruct((bsz, SUBLANES, d), F32)],
        scratch_shapes=[pltpu.VMEM((SUBLANES, d), F32)],
        compiler_params=_params("arbitrary", "arbitrary"),
        name="conv_prompt",
    )(x, mod, gpre.array, gpost.array, wbcx.array, cw.array, wout.array)


def _conv_sample_kernel(x_ref, st_ref, mod_ref, gpre_ref, gpost_ref, wbcx_ref, cw_ref, wout_ref, y_ref, ns_ref):
    x = x_ref[...]
    s, db, d = x.shape
    h = _flat(_pre(x, mod_ref, gpre_ref[...], 1, True)).astype(BF16)
    bcx = _dot(h, wbcx_ref[...])
    u = (bcx[:, d:2 * d] * bcx[:, 2 * d:]).reshape(s, db, d)
    ext = [st_ref[j] for j in range(CONV_W - 1)] + [u[t] for t in range(s)]
    y = jnp.stack([sum(cw_ref[j:j + 1, :] * ext[t + j] for j in range(CONV_W)) for t in range(s)], axis=0)
    for j in range(CONV_W - 1):
        ns_ref[j] = ext[s + j]
    o = _dot((bcx[:, :d] * _flat(y)).astype(BF16), wout_ref[...]).reshape(x.shape)
    y_ref[...] = _post(x, o, mod_ref, gpost_ref[...], 1, 1.0, True)


def _conv_sample(x, state, mod, layer, gpre, gpost, wbcx, cw, wout):
    r = _Rows(x, mod, layer)
    st_spec = pl.BlockSpec(state.shape, lambda b, t: (0, 0, 0))
    return pl.pallas_call(
        _conv_sample_kernel,
        grid=r.grid,
        in_specs=[r.x_spec, st_spec, r.mod_spec, _w_spec(gpre), _w_spec(gpost), _w_spec(wbcx), _w_spec(cw), _w_spec(wout)],
        out_specs=[r.x_spec, st_spec],
        out_shape=[jax.ShapeDtypeStruct(x.shape, F32), jax.ShapeDtypeStruct(state.shape, F32)],
        compiler_params=_params("arbitrary", "arbitrary"),
        name="conv_sample",
    )(x, state, mod, gpre.array, gpost.array, wbcx.array, cw.array, wout.array)


def _pool_mix(diff, wp_ref, scale):
    g = diff.shape[-1] // len(POOL_WINDOWS)
    parts = [_dot(diff[:, i * g:(i + 1) * g].astype(BF16), wp_ref[i]) for i in range(len(POOL_WINDOWS))]
    return jnp.concatenate(parts, axis=-1) * scale


def _pool_prompt_kernel(x_ref, mod_ref, gpre_ref, gpost_ref, wp_ref, ps_ref, *refs):
    ffn_refs, (y_ref, tail_ref, carry_ref) = refs[:-3], refs[-3:]
    it = pl.program_id(1)

    @pl.when(it == 0)
    def _():
        carry_ref[...] = jnp.zeros_like(carry_ref)

    x = x_ref[...]
    tm, d = x.shape
    g = d // len(POOL_WINDOWS)
    h = _pre(x, mod_ref, gpre_ref[...], 1, False)
    sums = [h]
    for k in range(len(POOL_WINDOWS)):
        cur = sums[-1]
        sums.append(cur + _shift_rows(cur, carry_ref[k], 2 ** k))
        carry_ref[k] = cur[tm - SUBLANES:, :]
    assert POOL_WINDOWS == tuple(2 ** (k + 1) for k in range(len(POOL_WINDOWS)))
    total = jnp.concatenate([sums[k + 1][:, k * g:(k + 1) * g] for k in range(len(POOL_WINDOWS))], axis=-1)
    lane = lax.broadcasted_iota(jnp.int32, (tm, d), 1)
    pos = lax.broadcasted_iota(jnp.int32, (tm, d), 0) + it * tm
    window = jnp.left_shift(2, lane // g)
    count = jnp.minimum(pos + 1, window).astype(F32)
    diff = total / count - h
    tail_ref[...] = h[tm - 2 * SUBLANES:, :]
    x = _post(x, _pool_mix(diff, wp_ref, ps_ref[...]), mod_ref, gpost_ref[...], 1, 1.0, False)
    y_ref[...] = _ffn_apply(x, mod_ref, ffn_refs, 2, False)


def _pool_prompt(x, mod, layer, gpre, gpost, wp, ps, *ffn_params):
    r = _Rows(x, mod, layer)
    bsz, _, d = x.shape
    return pl.pallas_call(
        _pool_prompt_kernel,
        grid=r.grid,
        in_specs=[r.x_spec, r.mod_spec] + [_w_spec(p) for p in (gpre, gpost, wp, ps) + ffn_params],
        out_specs=[r.x_spec, r.tail_spec(2 * SUBLANES, d)],
        out_shape=[jax.ShapeDtypeStruct(x.shape, F32), jax.ShapeDtypeStruct((bsz, 2 * SUBLANES, d), F32)],
        scratch_shapes=[pltpu.VMEM((len(POOL_WINDOWS), SUBLANES, d), F32)],
        compiler_params=_params("arbitrary", "arbitrary"),
        name="pool_ffn_prompt",
    )(x, mod, *(p.array for p in (gpre, gpost, wp, ps) + ffn_params))


def _pool_sample_kernel(x_ref, st_ref, mod_ref, gpre_ref, gpost_ref, wp_ref, ps_ref, *refs, past_len):
    ffn_refs, (y_ref, ns_ref) = refs[:-2], refs[-2:]
    x = x_ref[...]
    s, db, d = x.shape
    g = d // len(POOL_WINDOWS)
    hist = st_ref.shape[0]
    h = _pre(x, mod_ref, gpre_ref[...], 1, True)
    ext = [st_ref[j] for j in range(hist)] + [h[t] for t in range(s)]
    means = []
    for t in range(s):
        parts = []
        for k, w in enumerate(POOL_WINDOWS):
            sl = slice(k * g, (k + 1) * g)
            tot = ext[hist + t][:, sl]
            for j in range(1, w):
                tot = tot + ext[hist + t - j][:, sl]
            parts.append(tot / float(min(past_len + t + 1, w)))
        means.append(jnp.concatenate(parts, axis=-1))
    diff = _flat(jnp.stack(means, axis=0) - h)
    for j in range(hist):
        ns_ref[j] = ext[s + j]
    mixed = _pool_mix(diff, wp_ref, ps_ref[...]).reshape(x.shape)
    x = _post(x, mixed, mod_ref, gpost_ref[...], 1, 1.0, True)
    y_ref[...] = _ffn_apply(x, mod_ref, ffn_refs, 2, True)


def _pool_sample(x, state, mod, layer, past_len, gpre, gpost, wp, ps, *ffn_params):
    r = _Rows(x, mod, layer)
    st_spec = pl.BlockSpec(state.shape, lambda b, t: (0, 0, 0))
    return pl.pallas_call(
        functools.partial(_pool_sample_kernel, past_len=past_len),
        grid=r.grid,
        in_specs=[r.x_spec, st_spec, r.mod_spec] + [_w_spec(p) for p in (gpre, gpost, wp, ps) + ffn_params],
        out_specs=[r.x_spec, st_spec],
        out_shape=[jax.ShapeDtypeStruct(x.shape, F32), jax.ShapeDtypeStruct(state.shape, F32)],
        compiler_params=_params("arbitrary", "arbitrary"),
        name="pool_ffn_sample",
    )(x, state, mod, *(p.array for p in (gpre, gpost, wp, ps) + ffn_params))


def kernel(x_prompt, x_sample, cache_k, cache_v, state_conv, state_pool, page_table, c_prompt, c_sample,
           w_ada, b_ada, g_pre, g_post, w_ffn_gate, w_ffn_up, w_ffn_down, w_qkv, w_o, sb_bias, w_bcx,
           conv_w, w_conv_out, w_pool, pool_scale):
    bsz, seq, d = x_prompt.shape
    db, dec_seq, _ = x_sample.shape
    depth = w_ada.shape[0]
    head_dim = d // N_HEADS
    n_pages = page_table.shape[1]
    past_len = n_pages * PAGE_SIZE
    assert dec_seq <= 2 * SUBLANES and db % SUBLANES == 0

    mod_all = _ada(jnp.concatenate([c_sample, c_prompt], axis=0), w_ada, b_ada)
    mod_p = jnp.swapaxes(mod_all[:, :, db:], 1, 2)

    wg, wu, wd = (w.astype(BF16) for w in (w_ffn_gate, w_ffn_up, w_ffn_down))
    wqkv, wo, wbcx, wco, wpl = (w.astype(BF16) for w in (w_qkv, w_o, w_bcx, w_conv_out, w_pool))
    pages = lambda c: jnp.transpose(c, (0, 1, 3, 4, 2)).reshape(c.shape[0], c.shape[1], d, PAGE_SIZE)
    cache_k4, cache_v4 = pages(cache_k), pages(cache_v)

    xp = x_prompt
    xs = jnp.swapaxes(x_sample, 0, 1)
    st_conv = jnp.swapaxes(state_conv, 1, 2)
    st_pool = jnp.swapaxes(state_pool, 1, 2)

    conv_p, pool_p, k_s, v_s, conv_s, pool_s = ([] for _ in range(6))
    kv_t = ()
    g_pre4 = g_pre.reshape(depth, N_SUB, 1, d)
    g_post4 = g_post.reshape(depth, N_SUB, 1, d)
    pool_scale3 = pool_scale.reshape(pool_scale.shape[0], 1, d)
    for i in range(depth):
        gp = lambda j: _At(g_pre4, i, j)
        gq = lambda j: _At(g_post4, i, j)
        kind, slot = i % N_MIXERS, i // N_MIXERS
        ffn_w = lambda j: (_At(wg, i, j), _At(wu, i, j), _At(wd, i, j))

        xp = _ffn(xp, mod_p, i, 0, gp(0), gq(0), *ffn_w(0))
        xs = _ffn(xs, mod_all, i, 0, gp(0), gq(0), *ffn_w(0))

        ffn2 = (gp(2), gq(2), *ffn_w(1))
        if kind == 0:
            q, kb, vb, kv_t = _qkv_prompt(xp, mod_p, i, gp(1), _At(wqkv, slot), slot, w_qkv.shape[0], kv_t)
            o = _attn_prompt(q, kb, vb, sb_bias[slot])
            xp = _out_proj(xp, o, mod_p, i, gq(1), _At(wo, slot), *ffn2)

            q, k, v = _qkv_sample(xs, mod_all, i, gp(1), _At(wqkv, slot))
            to_seq = lambda a: jnp.swapaxes(a, 0, 1)
            o = _attn_sample(q, k, v, cache_k4, cache_v4, page_table, sb_bias[slot], slot)
            xs = _out_proj(xs, to_seq(o), mod_all, i, gq(1), _At(wo, slot), *ffn2)
            k_s.append(to_seq(k).reshape(db, dec_seq, N_HEADS, head_dim))
            v_s.append(to_seq(v).reshape(db, dec_seq, N_HEADS, head_dim))
        elif kind == 1:
            conv_ws = (_At(wbcx, slot), _At(conv_w, slot), _At(wco, slot))
            xp, tail = _conv_prompt(xp, mod_p, i, gp(1), gq(1), *conv_ws)
            conv_p.append(tail[:, SUBLANES - (CONV_W - 1):])
            xs, ns = _conv_sample(xs, st_conv[slot], mod_all, i, gp(1), gq(1), *conv_ws)
            conv_s.append(jnp.swapaxes(ns, 0, 1))
            xp = _ffn(xp, mod_p, i, 2, *ffn2)
            xs = _ffn(xs, mod_all, i, 2, *ffn2)
        else:
            pool_ws = (_At(wpl, slot), _At(pool_scale3, slot))
            xp, tail = _pool_prompt(xp, mod_p, i, gp(1), gq(1), *pool_ws, *ffn2)
            pool_p.append(tail[:, 2 * SUBLANES - POOL_HIST:])
            xs, ns = _pool_sample(xs, st_pool[slot], mod_all, i, past_len, gp(1), gq(1), *pool_ws, *ffn2)
            pool_s.append(jnp.swapaxes(ns, 0, 1))

    heads = lambda a: jnp.transpose(a.reshape(a.shape[0], bsz, N_HEADS, head_dim, seq), (0, 1, 4, 2, 3))
    return (xp, jnp.swapaxes(xs, 0, 1), heads(kv_t[0]), heads(kv_t[1]), jnp.stack(conv_p), jnp.stack(pool_p),
            jnp.stack(k_s), jnp.stack(v_s), jnp.stack(conv_s), jnp.stack(pool_s))
```

```python
import functools

import jax
import jax.numpy as jnp
from jax import lax
from jax.experimental import pallas as pl
from jax.experimental.pallas import tpu as pltpu

F32 = jnp.float32
BF16 = jnp.bfloat16

NEG_LOG2E = -1.4426950408889634
MASKED_LOGIT = -1e30
RMS_EPS = 1e-6
FFN_HALF = 0.5
N_SUB = 3
N_MOD = 3
N_MIXERS = 3
N_HEADS = 16
CONV_W = 3
POOL_WINDOWS = (2, 4, 8, 16)
POOL_HIST = max(POOL_WINDOWS) - 1
PAGE_SIZE = 128

SUBLANES = 8
LANES = 128
ROW_TILE = 512
FFN_ROW_TILE = 1024
FFN_CHAIN_ROWS = 256
ATTN_TILE = 512
ATTN_SUB = 256
ATTN_PAIRS = 4
SAMPLE_PAGES_PER_STEP = 16
VMEM_LIMIT = 56 * 1024 * 1024


def _params(*sem):
    return pltpu.CompilerParams(dimension_semantics=sem, vmem_limit_bytes=VMEM_LIMIT)


def _const_spec(shape):
    zeros = (0,) * len(shape)
    return pl.BlockSpec(shape, lambda *_: zeros)


def _rms(x, g):
    return x * lax.rsqrt(jnp.mean(x * x, axis=-1, keepdims=True) + RMS_EPS) * g


def _mod(mod_ref, j, m, per_row):
    k = j * N_MOD + m
    return mod_ref[k] if per_row else mod_ref[k:k + 1, :]


def _pre(x, mod_ref, g, j, per_row):
    return _rms(x, g) * (1.0 + _mod(mod_ref, j, 1, per_row)) + _mod(mod_ref, j, 0, per_row)


def _post(x, o, mod_ref, g, j, weight, per_row):
    return x + (weight * _mod(mod_ref, j, 2, per_row)) * _rms(o, g)


def _flat(x):
    return x.reshape(x.shape[0] * x.shape[1], x.shape[2]) if x.ndim == 3 else x


def _dot(a, b):
    return jnp.dot(a, b, preferred_element_type=F32)


def _softplus(z):
    return jnp.maximum(z, 0.0) + jnp.log(1.0 + jnp.exp2(jnp.abs(z) * NEG_LOG2E))


def _ada_kernel(c_ref, w_ref, b_ref, o_ref):
    c = c_ref[...]
    c_act = (c * jax.nn.sigmoid(c)).astype(BF16)
    o_ref[...] = _dot(c_act, w_ref[...].astype(BF16)) + b_ref[...]


def _ada(c_all, w_ada, b_ada):
    depth, d, n = w_ada.shape
    n_comp = n // d
    rows = c_all.shape[0]
    return pl.pallas_call(
        _ada_kernel,
        grid=(depth, n_comp),
        in_specs=[
            _const_spec((rows, d)),
            pl.BlockSpec((None, d, d), lambda l, c: (l, 0, c)),
            pl.BlockSpec((None, None, 1, d), lambda l, c: (l, c, 0, 0)),
        ],
        out_specs=pl.BlockSpec((None, None, rows, d), lambda l, c: (l, c, 0, 0)),
        out_shape=jax.ShapeDtypeStruct((depth, n_comp, rows, d), F32),
        compiler_params=_params("arbitrary", "arbitrary"),
        name="ada_mod",
    )(c_all, w_ada, b_ada.reshape(depth, n_comp, 1, d))


class _Rows:
    def __init__(self, x, mod, layer, tile=None):
        self.per_row = mod.shape[1] == N_SUB * N_MOD and mod.shape[2] != N_SUB * N_MOD
        d = x.shape[-1]
        if self.per_row:
            s, db, _ = x.shape
            self.grid = (1, 1)
            self.x_spec = pl.BlockSpec((s, db, d), lambda b, t: (0, 0, 0))
            self.mod_spec = pl.BlockSpec((None, N_SUB * N_MOD, db, d), lambda b, t: (layer, 0, 0, 0))
            self.tm = s * db
        else:
            bsz, t, _ = x.shape
            self.tm = min(tile or ROW_TILE, t)
            self.grid = (bsz, t // self.tm)
            self.x_spec = pl.BlockSpec((None, self.tm, d), lambda b, t: (b, t, 0))
            self.mod_spec = pl.BlockSpec((None, None, N_SUB * N_MOD, d), lambda b, t: (layer, b, 0, 0))

    def like_x(self, width):
        shape = self.x_spec.block_shape[:-1] + (width,)
        return pl.BlockSpec(shape, self.x_spec.index_map)

    def tail_spec(self, rows, d):
        return pl.BlockSpec((None, rows, d), lambda b, t: (b, 0, 0))


class _At:
    def __init__(self, array, *lead):
        self.array, self.lead = array, lead


def _w_spec(w):
    shape = w.array.shape
    block = (None,) * len(w.lead) + shape[len(w.lead):]
    index = tuple(w.lead) + (0,) * (len(shape) - len(w.lead))
    return pl.BlockSpec(block, lambda *_: index, pipeline_mode=pl.Buffered(1))


def _ffn_apply(x, mod_ref, ffn_refs, j, per_row):
    if not per_row and x.shape[0] > FFN_CHAIN_ROWS:
        parts = [_ffn_apply(x[r:r + FFN_CHAIN_ROWS], mod_ref, ffn_refs, j, per_row)
                 for r in range(0, x.shape[0], FFN_CHAIN_ROWS)]
        return jnp.concatenate(parts, axis=0)
    gpre_ref, gpost_ref, wg_ref, wu_ref, wd_ref = ffn_refs
    h = _flat(_pre(x, mod_ref, gpre_ref[...], j, per_row)).astype(BF16)
    g = _dot(h, wg_ref[...])
    u = _dot(h, wu_ref[...])
    a = ((g * jax.nn.sigmoid(g)) * u).astype(BF16)
    y = _dot(a, wd_ref[...]).reshape(x.shape)
    return _post(x, y, mod_ref, gpost_ref[...], j, FFN_HALF, per_row)


def _ffn_kernel(x_ref, mod_ref, *refs, j, per_row):
    refs[-1][...] = _ffn_apply(x_ref[...], mod_ref, refs[:-1], j, per_row)


def _ffn(x, mod, layer, j, gpre, gpost, wg, wu, wd):
    r = _Rows(x, mod, layer, FFN_ROW_TILE)
    d = x.shape[-1]
    return pl.pallas_call(
        functools.partial(_ffn_kernel, j=j, per_row=r.per_row),
        grid=r.grid,
        in_specs=[r.x_spec, r.mod_spec, _w_spec(gpre), _w_spec(gpost), _w_spec(wg), _w_spec(wu), _w_spec(wd)],
        out_specs=r.x_spec,
        out_shape=jax.ShapeDtypeStruct(x.shape, F32),
        compiler_params=_params("arbitrary", "arbitrary"),
        name="ffn",
    )(x, mod, gpre.array, gpost.array, wg.array, wu.array, wd.array)


def _qkv_kernel(x_ref, mod_ref, gpre_ref, w_ref, *refs, per_row, q_scale, n_prev):
    prev_k, prev_v = refs[:n_prev], refs[n_prev:2 * n_prev]
    q_ref, k_ref, v_ref, *bf16_refs = refs[2 * n_prev:]
    x = x_ref[...]
    d = x.shape[-1]
    h = _flat(_pre(x, mod_ref, gpre_ref[...], 1, per_row)).astype(BF16)
    qkv = _dot(h, w_ref[...])
    k = qkv[:, d:2 * d]
    v = qkv[:, 2 * d:]
    q_ref[...] = (qkv[:, :d] * q_scale).reshape(q_ref.shape).astype(q_ref.dtype)
    if bf16_refs:
        if k_ref.ndim == 3:
            for s in range(n_prev):
                k_ref[s] = prev_k[s][...]
                v_ref[s] = prev_v[s][...]
            k_ref[n_prev] = k.T
            v_ref[n_prev] = v.T
        else:
            k_ref[...] = k.T
            v_ref[...] = v.T
        bf16_refs[0][...] = k.astype(BF16)
        bf16_refs[1][...] = v.astype(BF16)
    else:
        k_ref[...] = k.reshape(k_ref.shape)
        v_ref[...] = v.reshape(v_ref.shape)


def _qkv_sample(x, mod, layer, gpre, w):
    r = _Rows(x, mod, layer)
    d = x.shape[-1]
    return pl.pallas_call(
        functools.partial(_qkv_kernel, per_row=True, q_scale=(d // N_HEADS) ** -0.5, n_prev=0),
        grid=r.grid,
        in_specs=[r.x_spec, r.mod_spec, _w_spec(gpre), _w_spec(w)],
        out_specs=[r.x_spec] * 3,
        out_shape=[jax.ShapeDtypeStruct(x.shape, F32)] * 3,
        compiler_params=_params("arbitrary", "arbitrary"),
        name="qkv_proj_sample",
    )(x, mod, gpre.array, w.array)


def _qkv_prompt(x, mod, layer, gpre, w, slot, n_slots, prev):
    r = _Rows(x, mod, layer)
    bsz, t, d = x.shape
    last = slot == n_slots - 1
    prev_k, prev_v = ([p[0] for p in prev], [p[1] for p in prev]) if last else ([], [])
    tile = pl.BlockSpec((None, d, r.tm), lambda b, i: (b, 0, i))
    if last:
        t_shape = jax.ShapeDtypeStruct((n_slots, bsz, d, t), F32)
        t_spec = pl.BlockSpec((n_slots, None, d, r.tm), lambda b, i: (0, b, 0, i))
    else:
        t_shape, t_spec = jax.ShapeDtypeStruct((bsz, d, t), F32), tile
    q, kt, vt, kb, vb = pl.pallas_call(
        functools.partial(_qkv_kernel, per_row=False, q_scale=(d // N_HEADS) ** -0.5, n_prev=len(prev_k)),
        grid=r.grid,
        in_specs=[r.x_spec, r.mod_spec, _w_spec(gpre), _w_spec(w)] + [tile] * (2 * len(prev_k)),
        out_specs=[r.x_spec, t_spec, t_spec, r.x_spec, r.x_spec],
        out_shape=[jax.ShapeDtypeStruct(x.shape, BF16), t_shape, t_shape,
                   jax.ShapeDtypeStruct(x.shape, BF16), jax.ShapeDtypeStruct(x.shape, BF16)],
        compiler_params=_params("arbitrary", "arbitrary"),
        name="qkv_proj_prompt",
    )(x, mod, gpre.array, w.array, *prev_k, *prev_v)
    return q, kb, vb, (kt, vt)


def _out_proj_kernel(x_ref, o_ref, mod_ref, gpost_ref, w_ref, *refs, per_row):
    def chain(rows):
        x = x_ref[rows]
        y = _dot(_flat(o_ref[rows]).astype(BF16), w_ref[...]).reshape(x.shape)
        x = _post(x, y, mod_ref, gpost_ref[...], 1, 1.0, per_row)
        refs[-1][rows] = _ffn_apply(x, mod_ref, refs[:-1], 2, per_row)

    if per_row:
        chain(slice(None))
    else:
        for r in range(0, x_ref.shape[0], FFN_CHAIN_ROWS):
            chain(slice(r, r + FFN_CHAIN_ROWS))


def _out_proj(x, o, mod, layer, gpost, w, *ffn_params):
    r = _Rows(x, mod, layer, FFN_ROW_TILE)
    return pl.pallas_call(
        functools.partial(_out_proj_kernel, per_row=r.per_row),
        grid=r.grid,
        in_specs=[r.x_spec, r.x_spec, r.mod_spec, _w_spec(gpost), _w_spec(w)] + [_w_spec(p) for p in ffn_params],
        out_specs=r.x_spec,
        out_shape=jax.ShapeDtypeStruct(x.shape, F32),
        compiler_params=_params("arbitrary", "arbitrary"),
        name="out_proj_ffn",
    )(x, o, mod, gpost.array, w.array, *(p.array for p in ffn_params))


def _suffix_ones(n):
    j = lax.broadcasted_iota(jnp.int32, (n, n), 0)
    s = lax.broadcasted_iota(jnp.int32, (n, n), 1)
    return (j >= s).astype(BF16)


def _dot_nt(a, b):
    return lax.dot_general(a, b, (((1,), (1,)), ((), ())), preferred_element_type=F32)


def _sb_logits(z, uu, mask):
    sp = _softplus(z)
    if mask is not None:
        sp = jnp.where(mask, sp, 0.0)
        z = jnp.where(mask, z, MASKED_LOGIT)
    sub = uu.shape[1]
    parts, totals = [], []
    for s in range(z.shape[1] // sub):
        cols = slice(s * sub, (s + 1) * sub)
        suffix = _dot(sp[:, cols].astype(BF16), uu)
        parts.append(z[:, cols] - suffix)
        totals.append(jnp.broadcast_to(suffix[:, 0:1], (z.shape[0], LANES)))
    return (parts[0] if len(parts) == 1 else jnp.concatenate(parts, axis=1)), totals


def _sb_weights(t, totals, run_ref):
    run = run_ref[...]
    sub = t.shape[1] // len(totals)
    args = [None] * len(totals)
    for s in reversed(range(len(totals))):
        args[s] = t[:, s * sub:(s + 1) * sub] - jnp.concatenate([run] * (sub // LANES), axis=1)
        run = run + totals[s]
    run_ref[...] = run
    return jnp.exp(args[0] if len(args) == 1 else jnp.concatenate(args, axis=1)).astype(BF16)


def _attn_prompt_kernel(bias_ref, q_ref, k_ref, v_ref, uu_ref, o_ref, run_ref, acc_ref, *, head_dim):
    group = pl.program_id(1)
    i = pl.program_id(2)
    tq = q_ref.shape[0]
    pair = 2 * head_dim
    n_pairs = q_ref.shape[1] // pair
    lane = lax.broadcasted_iota(jnp.int32, (tq, pair), 1)
    row = lax.broadcasted_iota(jnp.int32, (2 * tq, tq), 0)
    col = lax.broadcasted_iota(jnp.int32, (2 * tq, tq), 1)
    mask = col < jnp.where(row < tq, row, row - tq)
    uu = uu_ref[...]
    run_ref[...] = jnp.zeros_like(run_ref)
    acc_ref[...] = jnp.zeros_like(acc_ref)

    def keys(j):
        return pl.ds(pl.multiple_of(j * tq, tq), tq)

    def stacked_q(p):
        q = q_ref[:, p * pair:(p + 1) * pair]
        zero = jnp.zeros_like(q)
        return jnp.concatenate([jnp.where(lane < head_dim, q, zero), jnp.where(lane >= head_dim, q, zero)], axis=0)

    qqs = [stacked_q(p) for p in range(n_pairs)]

    def block(j, mask):
        for p in range(n_pairs):
            lanes = slice(p * pair, (p + 1) * pair)
            z = _dot_nt(qqs[p], k_ref[keys(j), lanes])
            head = 2 * (group * n_pairs + p)
            z = jnp.concatenate([z[:tq] + bias_ref[head], z[tq:] + bias_ref[head + 1]], axis=0)
            a = _sb_weights(*_sb_logits(z, uu, mask), run_ref.at[p])
            acc_ref[p] += _dot(a, v_ref[keys(j), lanes])

    block(i, mask)

    def body(n, carry):
        block(i - 1 - n, None)
        return carry

    lax.fori_loop(0, i, body, 0)
    for p in range(n_pairs):
        acc = acc_ref[p]
        o_ref[:, p * pair:(p + 1) * pair] = jnp.where(lane < head_dim, acc[:tq], acc[tq:]).astype(o_ref.dtype)


def _attn_prompt(q, kb, vb, bias):
    bsz, t, d = q.shape
    tq = min(ATTN_TILE, t)
    sub = min(ATTN_SUB, tq)
    head_dim = d // N_HEADS
    pair = 2 * head_dim
    width = ATTN_PAIRS * pair
    blk = pl.BlockSpec((None, tq, width), lambda b, g, i: (b, i, g))
    seq = pl.BlockSpec((None, t, width), lambda b, g, i: (b, 0, g))
    return pl.pallas_call(
        functools.partial(_attn_prompt_kernel, head_dim=head_dim),
        grid=(bsz, d // width, t // tq),
        in_specs=[pl.BlockSpec(memory_space=pltpu.SMEM), blk, seq, seq,
                  pl.BlockSpec((sub, sub), lambda b, g, i: (0, 0))],
        out_specs=blk,
        out_shape=jax.ShapeDtypeStruct(q.shape, BF16),
        scratch_shapes=[pltpu.VMEM((ATTN_PAIRS, 2 * tq, LANES), F32), pltpu.VMEM((ATTN_PAIRS, 2 * tq, pair), F32)],
        compiler_params=_params("arbitrary", "arbitrary", "arbitrary"),
        name="sb_attn_prompt",
    )(bias, q, kb, vb, _suffix_ones(sub))


def _attn_sample_kernel(pt_ref, brow_ref, q_ref, k_ref, v_ref, *rest, head_dim, pages_per_step):
    pg = pages_per_step
    kp_refs, vp_refs = rest[:pg], rest[pg:2 * pg]
    uu_ref, o_ref, qbd_ref, knew_ref, vnew_ref, run_ref, acc_ref = rest[2 * pg:]
    b = pl.program_id(0)
    p = pl.program_id(1)
    s, _, d = q_ref.shape
    rows = s * N_HEADS
    row = lax.broadcasted_iota(jnp.int32, (rows, d), 0)
    lane = lax.broadcasted_iota(jnp.int32, (rows, d), 1)
    own_head = (lane // head_dim) == (row % N_HEADS)
    bias = brow_ref[...]
    uu = uu_ref[...]

    @pl.when(p == 0)
    def _():
        rep = jnp.concatenate([jnp.broadcast_to(q_ref[t, pl.ds(b, 1), :], (N_HEADS, d)) for t in range(s)], axis=0)
        qbd_ref[...] = jnp.where(own_head, rep, 0.0).astype(BF16)
        run_ref[...] = jnp.zeros_like(run_ref)
        acc_ref[...] = jnp.zeros_like(acc_ref)
        pad = 2 * SUBLANES
        r16 = lax.broadcasted_iota(jnp.int32, (pad, d), 0)
        kn = jnp.zeros((pad, d), F32)
        vn = jnp.zeros((pad, d), F32)
        for t in range(s):
            kn = jnp.where(r16 == t, jnp.broadcast_to(k_ref[t, pl.ds(b, 1), :], (pad, d)), kn)
            vn = jnp.where(r16 == t, jnp.broadcast_to(v_ref[t, pl.ds(b, 1), :], (pad, d)), vn)
        knew_ref[...] = jnp.zeros_like(knew_ref)
        vnew_ref[...] = jnp.zeros_like(vnew_ref)
        knew_ref[0:pad, :] = kn.astype(BF16)
        vnew_ref[0:pad, :] = vn.astype(BF16)
        r2 = lax.broadcasted_iota(jnp.int32, (rows, PAGE_SIZE), 0)
        c2 = lax.broadcasted_iota(jnp.int32, (rows, PAGE_SIZE), 1)
        new = _sb_logits(_dot_nt(qbd_ref[...], knew_ref[...]) + bias, uu, c2 < r2 // N_HEADS)
        acc_ref[...] += _dot(_sb_weights(*new, run_ref), vnew_ref[...])

    qbd = qbd_ref[...]
    z = jnp.concatenate([_dot(qbd, kp_refs[pg - 1 - c][...].astype(BF16)) + bias for c in range(pg)], axis=1)
    a = _sb_weights(*_sb_logits(z, uu, None), run_ref)
    acc = acc_ref[...]
    for c in range(pg):
        acc = acc + _dot_nt(a[:, c * PAGE_SIZE:(c + 1) * PAGE_SIZE], vp_refs[pg - 1 - c][...].astype(BF16))
    acc_ref[...] = acc

    @pl.when(p == pl.num_programs(1) - 1)
    def _():
        own = jnp.where(own_head, acc_ref[...], 0.0)
        o_ref[...] = jnp.sum(own.reshape(s, N_HEADS, d), axis=1)


def _attn_sample(q, k_new, v_new, cache_k, cache_v, page_table, bias, slot):
    s, db, d = q.shape
    n_pages = page_table.shape[1]
    head_dim = d // N_HEADS
    rows = s * N_HEADS
    pg = max(g for g in range(1, SAMPLE_PAGES_PER_STEP + 1) if n_pages % g == 0)
    bias_rows = jnp.broadcast_to(jnp.tile(bias, s)[:, None], (rows, PAGE_SIZE)).astype(F32)

    def page_spec(g):
        def page_map(b, p, pt):
            return (slot, pt[b * n_pages + n_pages - 1 - (p * pg + g)], 0, 0)
        return pl.BlockSpec((None, None, d, PAGE_SIZE), page_map)

    const = lambda shape: pl.BlockSpec(shape, lambda b, p, pt: (0,) * len(shape), pipeline_mode=pl.Buffered(1))
    grid_spec = pltpu.PrefetchScalarGridSpec(
        num_scalar_prefetch=1,
        grid=(db, n_pages // pg),
        in_specs=[const((rows, PAGE_SIZE)), const(q.shape), const(q.shape), const(q.shape)]
        + [page_spec(g) for g in range(pg)] * 2
        + [const((PAGE_SIZE, PAGE_SIZE))],
        out_specs=pl.BlockSpec((None, s, d), lambda b, p, pt: (b, 0, 0)),
        scratch_shapes=[
            pltpu.VMEM((rows, d), BF16),
            pltpu.VMEM((PAGE_SIZE, d), BF16),
            pltpu.VMEM((PAGE_SIZE, d), BF16),
            pltpu.VMEM((rows, PAGE_SIZE), F32),
            pltpu.VMEM((rows, d), F32),
        ],
    )
    return pl.pallas_call(
        functools.partial(_attn_sample_kernel, head_dim=head_dim, pages_per_step=pg),
        grid_spec=grid_spec,
        out_shape=jax.ShapeDtypeStruct((db, s, d), F32),
        compiler_params=_params("arbitrary", "arbitrary"),
        name="sb_attn_sample",
    )(page_table.reshape(-1), bias_rows, q, k_new, v_new, *([cache_k] * pg), *([cache_v] * pg),
      _suffix_ones(PAGE_SIZE))


def _shift_rows(u, prev, s):
    if s == SUBLANES:
        return jnp.concatenate([prev, u[:-SUBLANES]], axis=0)
    ru = pltpu.roll(u, s, axis=0)
    rp = pltpu.roll(prev, s, axis=0)
    r8 = lax.broadcasted_iota(jnp.int32, prev.shape, 0)
    return jnp.concatenate([jnp.where(r8 < s, rp, ru[:SUBLANES]), ru[SUBLANES:]], axis=0)


def _conv_prompt_kernel(x_ref, mod_ref, gpre_ref, gpost_ref, wbcx_ref, cw_ref, wout_ref, y_ref, tail_ref, carry_ref):
    @pl.when(pl.program_id(1) == 0)
    def _():
        carry_ref[...] = jnp.zeros_like(carry_ref)

    x = x_ref[...]
    d = x.shape[-1]
    h = _pre(x, mod_ref, gpre_ref[...], 1, False).astype(BF16)
    bcx = _dot(h, wbcx_ref[...])
    u = bcx[:, d:2 * d] * bcx[:, 2 * d:]
    prev = carry_ref[...]
    y = cw_ref[0:1, :] * _shift_rows(u, prev, 2) + cw_ref[1:2, :] * _shift_rows(u, prev, 1) + cw_ref[2:3, :] * u
    last = u[u.shape[0] - SUBLANES:, :]
    carry_ref[...] = last
    tail_ref[...] = last
    o = _dot((bcx[:, :d] * y).astype(BF16), wout_ref[...])
    y_ref[...] = _post(x, o, mod_ref, gpost_ref[...], 1, 1.0, False)


def _conv_prompt(x, mod, layer, gpre, gpost, wbcx, cw, wout):
    r = _Rows(x, mod, layer)
    bsz, _, d = x.shape
    return pl.pallas_call(
        _conv_prompt_kernel,
        grid=r.grid,
        in_specs=[r.x_spec, r.mod_spec, _w_spec(gpre), _w_spec(gpost), _w_spec(wbcx), _w_spec(cw), _w_spec(wout)],
        out_specs=[r.x_spec, r.tail_spec(SUBLANES, d)],
        out_shape=[jax.ShapeDtypeStruct(x.shape, F32), jax.ShapeDtypeStruct((bsz, SUBLANES, d), F32)],
        scratch_shapes=[pltpu.VMEM((SUBLANES, d), F32)],
        compiler_params=_params("arbitrary", "arbitrary"),
        name="conv_prompt",
    )(x, mod, gpre.array, gpost.array, wbcx.array, cw.array, wout.array)


def _conv_sample_kernel(x_ref, st_ref, mod_ref, gpre_ref, gpost_ref, wbcx_ref, cw_ref, wout_ref, y_ref, ns_ref):
    x = x_ref[...]
    s, db, d = x.shape
    h = _flat(_pre(x, mod_ref, gpre_ref[...], 1, True)).astype(BF16)
    bcx = _dot(h, wbcx_ref[...])
    u = (bcx[:, d:2 * d] * bcx[:, 2 * d:]).reshape(s, db, d)
    ext = [st_ref[j] for j in range(CONV_W - 1)] + [u[t] for t in range(s)]
    y = jnp.stack([sum(cw_ref[j:j + 1, :] * ext[t + j] for j in range(CONV_W)) for t in range(s)], axis=0)
    for j in range(CONV_W - 1):
        ns_ref[j] = ext[s + j]
    o = _dot((bcx[:, :d] * _flat(y)).astype(BF16), wout_ref[...]).reshape(x.shape)
    y_ref[...] = _post(x, o, mod_ref, gpost_ref[...], 1, 1.0, True)


def _conv_sample(x, state, mod, layer, gpre, gpost, wbcx, cw, wout):
    r = _Rows(x, mod, layer)
    st_spec = pl.BlockSpec(state.shape, lambda b, t: (0, 0, 0))
    return pl.pallas_call(
        _conv_sample_kernel,
        grid=r.grid,
        in_specs=[r.x_spec, st_spec, r.mod_spec, _w_spec(gpre), _w_spec(gpost), _w_spec(wbcx), _w_spec(cw), _w_spec(wout)],
        out_specs=[r.x_spec, st_spec],
        out_shape=[jax.ShapeDtypeStruct(x.shape, F32), jax.ShapeDtypeStruct(state.shape, F32)],
        compiler_params=_params("arbitrary", "arbitrary"),
        name="conv_sample",
    )(x, state, mod, gpre.array, gpost.array, wbcx.array, cw.array, wout.array)


def _pool_mix(diff, wp_ref, scale):
    g = diff.shape[-1] // len(POOL_WINDOWS)
    parts = [_dot(diff[:, i * g:(i + 1) * g].astype(BF16), wp_ref[i]) for i in range(len(POOL_WINDOWS))]
    return jnp.concatenate(parts, axis=-1) * scale


def _pool_prompt_kernel(x_ref, mod_ref, gpre_ref, gpost_ref, wp_ref, ps_ref, *refs):
    ffn_refs, (y_ref, tail_ref, carry_ref) = refs[:-3], refs[-3:]
    it = pl.program_id(1)

    @pl.when(it == 0)
    def _():
        carry_ref[...] = jnp.zeros_like(carry_ref)

    x = x_ref[...]
    tm, d = x.shape
    g = d // len(POOL_WINDOWS)
    h = _pre(x, mod_ref, gpre_ref[...], 1, False)
    sums = [h]
    for k in range(len(POOL_WINDOWS)):
        cur = sums[-1]
        sums.append(cur + _shift_rows(cur, carry_ref[k], 2 ** k))
        carry_ref[k] = cur[tm - SUBLANES:, :]
    assert POOL_WINDOWS == tuple(2 ** (k + 1) for k in range(len(POOL_WINDOWS)))
    total = jnp.concatenate([sums[k + 1][:, k * g:(k + 1) * g] for k in range(len(POOL_WINDOWS))], axis=-1)
    lane = lax.broadcasted_iota(jnp.int32, (tm, d), 1)
    pos = lax.broadcasted_iota(jnp.int32, (tm, d), 0) + it * tm
    window = jnp.left_shift(2, lane // g)
    count = jnp.minimum(pos + 1, window).astype(F32)
    diff = total / count - h
    tail_ref[...] = h[tm - 2 * SUBLANES:, :]
    x = _post(x, _pool_mix(diff, wp_ref, ps_ref[...]), mod_ref, gpost_ref[...], 1, 1.0, False)
    y_ref[...] = _ffn_apply(x, mod_ref, ffn_refs, 2, False)


def _pool_prompt(x, mod, layer, gpre, gpost, wp, ps, *ffn_params):
    r = _Rows(x, mod, layer)
    bsz, _, d = x.shape
    return pl.pallas_call(
        _pool_prompt_kernel,
        grid=r.grid,
        in_specs=[r.x_spec, r.mod_spec] + [_w_spec(p) for p in (gpre, gpost, wp, ps) + ffn_params],
        out_specs=[r.x_spec, r.tail_spec(2 * SUBLANES, d)],
        out_shape=[jax.ShapeDtypeStruct(x.shape, F32), jax.ShapeDtypeStruct((bsz, 2 * SUBLANES, d), F32)],
        scratch_shapes=[pltpu.VMEM((len(POOL_WINDOWS), SUBLANES, d), F32)],
        compiler_params=_params("arbitrary", "arbitrary"),
        name="pool_ffn_prompt",
    )(x, mod, *(p.array for p in (gpre, gpost, wp, ps) + ffn_params))


def _pool_sample_kernel(x_ref, st_ref, mod_ref, gpre_ref, gpost_ref, wp_ref, ps_ref, *refs, past_len):
    ffn_refs, (y_ref, ns_ref) = refs[:-2], refs[-2:]
    x = x_ref[...]
    s, db, d = x.shape
    g = d // len(POOL_WINDOWS)
    hist = st_ref.shape[0]
    h = _pre(x, mod_ref, gpre_ref[...], 1, True)
    ext = [st_ref[j] for j in range(hist)] + [h[t] for t in range(s)]
    means = []
    for t in range(s):
        parts = []
        for k, w in enumerate(POOL_WINDOWS):
            sl = slice(k * g, (k + 1) * g)
            tot = ext[hist + t][:, sl]
            for j in range(1, w):
                tot = tot + ext[hist + t - j][:, sl]
            parts.append(tot / float(min(past_len + t + 1, w)))
        means.append(jnp.concatenate(parts, axis=-1))
    diff = _flat(jnp.stack(means, axis=0) - h)
    for j in range(hist):
        ns_ref[j] = ext[s + j]
    mixed = _pool_mix(diff, wp_ref, ps_ref[...]).reshape(x.shape)
    x = _post(x, mixed, mod_ref, gpost_ref[...], 1, 1.0, True)
    y_ref[...] = _ffn_apply(x, mod_ref, ffn_refs, 2, True)


def _pool_sample(x, state, mod, layer, past_len, gpre, gpost, wp, ps, *ffn_params):
    r = _Rows(x, mod, layer)
    st_spec = pl.BlockSpec(state.shape, lambda b, t: (0, 0, 0))
    return pl.pallas_call(
        functools.partial(_pool_sample_kernel, past_len=past_len),
        grid=r.grid,
        in_specs=[r.x_spec, st_spec, r.mod_spec] + [_w_spec(p) for p in (gpre, gpost, wp, ps) + ffn_params],
        out_specs=[r.x_spec, st_spec],
        out_shape=[jax.ShapeDtypeStruct(x.shape, F32), jax.ShapeDtypeStruct(state.shape, F32)],
        compiler_params=_params("arbitrary", "arbitrary"),
        name="pool_ffn_sample",
    )(x, state, mod, *(p.array for p in (gpre, gpost, wp, ps) + ffn_params))


def kernel(x_prompt, x_sample, cache_k, cache_v, state_conv, state_pool, page_table, c_prompt, c_sample,
           w_ada, b_ada, g_pre, g_post, w_ffn_gate, w_ffn_up, w_ffn_down, w_qkv, w_o, sb_bias, w_bcx,
           conv_w, w_conv_out, w_pool, pool_scale):
    bsz, seq, d = x_prompt.shape
    db, dec_seq, _ = x_sample.shape
    depth = w_ada.shape[0]
    head_dim = d // N_HEADS
    n_pages = page_table.shape[1]
    past_len = n_pages * PAGE_SIZE
    assert dec_seq <= 2 * SUBLANES and db % SUBLANES == 0

    mod_all = _ada(jnp.concatenate([c_sample, c_prompt], axis=0), w_ada, b_ada)
    mod_p = jnp.swapaxes(mod_all[:, :, db:], 1, 2)

    wg, wu, wd = (w.astype(BF16) for w in (w_ffn_gate, w_ffn_up, w_ffn_down))
    wqkv, wo, wbcx, wco, wpl = (w.astype(BF16) for w in (w_qkv, w_o, w_bcx, w_conv_out, w_pool))
    pages = lambda c: jnp.transpose(c, (0, 1, 3, 4, 2)).reshape(c.shape[0], c.shape[1], d, PAGE_SIZE)
    cache_k4, cache_v4 = pages(cache_k), pages(cache_v)

    xp = x_prompt
    xs = jnp.swapaxes(x_sample, 0, 1)
    st_conv = jnp.swapaxes(state_conv, 1, 2)
    st_pool = jnp.swapaxes(state_pool, 1, 2)

    conv_p, pool_p, k_s, v_s, conv_s, pool_s = ([] for _ in range(6))
    kv_t = []
    g_pre4 = g_pre.reshape(depth, N_SUB, 1, d)
    g_post4 = g_post.reshape(depth, N_SUB, 1, d)
    pool_scale3 = pool_scale.reshape(pool_scale.shape[0], 1, d)
    for i in range(depth):
        gp = lambda j: _At(g_pre4, i, j)
        gq = lambda j: _At(g_post4, i, j)
        kind, slot = i % N_MIXERS, i // N_MIXERS
        ffn_w = lambda j: (_At(wg, i, j), _At(wu, i, j), _At(wd, i, j))

        xp = _ffn(xp, mod_p, i, 0, gp(0), gq(0), *ffn_w(0))
        xs = _ffn(xs, mod_all, i, 0, gp(0), gq(0), *ffn_w(0))

        ffn2 = (gp(2), gq(2), *ffn_w(1))
        if kind == 0:
            q, kb, vb, kv = _qkv_prompt(xp, mod_p, i, gp(1), _At(wqkv, slot), slot, w_qkv.shape[0], kv_t)
            kv_t.append(kv)
            o = _attn_prompt(q, kb, vb, sb_bias[slot])
            xp = _out_proj(xp, o, mod_p, i, gq(1), _At(wo, slot), *ffn2)

            q, k, v = _qkv_sample(xs, mod_all, i, gp(1), _At(wqkv, slot))
            to_seq = lambda a: jnp.swapaxes(a, 0, 1)
            o = _attn_sample(q, k, v, cache_k4, cache_v4, page_table, sb_bias[slot], slot)
            xs = _out_proj(xs, to_seq(o), mod_all, i, gq(1), _At(wo, slot), *ffn2)
            k_s.append(to_seq(k).reshape(db, dec_seq, N_HEADS, head_dim))
            v_s.append(to_seq(v).reshape(db, dec_seq, N_HEADS, head_dim))
        elif kind == 1:
            conv_ws = (_At(wbcx, slot), _At(conv_w, slot), _At(wco, slot))
            xp, tail = _conv_prompt(xp, mod_p, i, gp(1), gq(1), *conv_ws)
            conv_p.append(tail[:, SUBLANES - (CONV_W - 1):])
            xs, ns = _conv_sample(xs, st_conv[slot], mod_all, i, gp(1), gq(1), *conv_ws)
            conv_s.append(jnp.swapaxes(ns, 0, 1))
            xp = _ffn(xp, mod_p, i, 2, *ffn2)
            xs = _ffn(xs, mod_all, i, 2, *ffn2)
        else:
            pool_ws = (_At(wpl, slot), _At(pool_scale3, slot))
            xp, tail = _pool_prompt(xp, mod_p, i, gp(1), gq(1), *pool_ws, *ffn2)
            pool_p.append(tail[:, 2 * SUBLANES - POOL_HIST:])
            xs, ns = _pool_sample(xs, st_pool[slot], mod_all, i, past_len, gp(1), gq(1), *pool_ws, *ffn2)
            pool_s.append(jnp.swapaxes(ns, 0, 1))

    heads = lambda a: jnp.transpose(a.reshape(a.shape[0], bsz, N_HEADS, head_dim, seq), (0, 1, 4, 2, 3))
    return (xp, jnp.swapaxes(xs, 0, 1), heads(kv_t[-1][0]), heads(kv_t[-1][1]), jnp.stack(conv_p), jnp.stack(pool_p),
            jnp.stack(k_s), jnp.stack(v_s), jnp.stack(conv_s), jnp.stack(pool_s))
```

```python
import functools

import jax
import jax.numpy as jnp
from jax import lax
from jax.experimental import pallas as pl
from jax.experimental.pallas import tpu as pltpu

F32 = jnp.float32
BF16 = jnp.bfloat16

NEG_LOG2E = -1.4426950408889634
MASKED_LOGIT = -1e30
RMS_EPS = 1e-6
FFN_HALF = 0.5
N_SUB = 3
N_MOD = 3
N_MIXERS = 3
N_HEADS = 16
CONV_W = 3
POOL_WINDOWS = (2, 4, 8, 16)
POOL_HIST = max(POOL_WINDOWS) - 1
PAGE_SIZE = 128

SUBLANES = 8
LANES = 128
ROW_TILE = 512
FFN_ROW_TILE = 1024
FFN_CHAIN_ROWS = 256
ATTN_TILE = 512
ATTN_SUB = 256
ATTN_PAIRS = 4
SAMPLE_PAGES_PER_STEP = 16
VMEM_LIMIT = 56 * 1024 * 1024


def _params(*sem):
    return pltpu.CompilerParams(dimension_semantics=sem, vmem_limit_bytes=VMEM_LIMIT)


def _const_spec(shape):
    zeros = (0,) * len(shape)
    return pl.BlockSpec(shape, lambda *_: zeros)


def _rms(x, g):
    return x * lax.rsqrt(jnp.mean(x * x, axis=-1, keepdims=True) + RMS_EPS) * g


def _mod(mod_ref, j, m, per_row):
    k = j * N_MOD + m
    return mod_ref[k] if per_row else mod_ref[k:k + 1, :]


def _pre(x, mod_ref, g, j, per_row):
    return _rms(x, g) * (1.0 + _mod(mod_ref, j, 1, per_row)) + _mod(mod_ref, j, 0, per_row)


def _post(x, o, mod_ref, g, j, weight, per_row):
    return x + (weight * _mod(mod_ref, j, 2, per_row)) * _rms(o, g)


def _flat(x):
    return x.reshape(x.shape[0] * x.shape[1], x.shape[2]) if x.ndim == 3 else x


def _dot(a, b):
    return jnp.dot(a, b, preferred_element_type=F32)


def _softplus(z):
    return jnp.maximum(z, 0.0) + jnp.log(1.0 + jnp.exp2(jnp.abs(z) * NEG_LOG2E))


def _ada_kernel(c_ref, w_ref, b_ref, o_ref):
    c = c_ref[...]
    c_act = (c * jax.nn.sigmoid(c)).astype(BF16)
    o_ref[...] = _dot(c_act, w_ref[...].astype(BF16)) + b_ref[...]


def _ada(c_all, w_ada, b_ada):
    depth, d, n = w_ada.shape
    n_comp = n // d
    rows = c_all.shape[0]
    return pl.pallas_call(
        _ada_kernel,
        grid=(depth, n_comp),
        in_specs=[
            _const_spec((rows, d)),
            pl.BlockSpec((None, d, d), lambda l, c: (l, 0, c)),
            pl.BlockSpec((None, None, 1, d), lambda l, c: (l, c, 0, 0)),
        ],
        out_specs=pl.BlockSpec((None, None, rows, d), lambda l, c: (l, c, 0, 0)),
        out_shape=jax.ShapeDtypeStruct((depth, n_comp, rows, d), F32),
        compiler_params=_params("arbitrary", "arbitrary"),
        name="ada_mod",
    )(c_all, w_ada, b_ada.reshape(depth, n_comp, 1, d))


class _Rows:
    def __init__(self, x, mod, layer, tile=None):
        self.per_row = mod.shape[1] == N_SUB * N_MOD and mod.shape[2] != N_SUB * N_MOD
        d = x.shape[-1]
        if self.per_row:
            s, db, _ = x.shape
            self.grid = (1, 1)
            self.x_spec = pl.BlockSpec((s, db, d), lambda b, t: (0, 0, 0))
            self.mod_spec = pl.BlockSpec((None, N_SUB * N_MOD, db, d), lambda b, t: (layer, 0, 0, 0))
            self.tm = s * db
        else:
            bsz, t, _ = x.shape
            self.tm = min(tile or ROW_TILE, t)
            self.grid = (bsz, t // self.tm)
            self.x_spec = pl.BlockSpec((None, self.tm, d), lambda b, t: (b, t, 0))
            self.mod_spec = pl.BlockSpec((None, None, N_SUB * N_MOD, d), lambda b, t: (layer, b, 0, 0))

    def like_x(self, width):
        shape = self.x_spec.block_shape[:-1] + (width,)
        return pl.BlockSpec(shape, self.x_spec.index_map)

    def tail_spec(self, rows, d):
        return pl.BlockSpec((None, rows, d), lambda b, t: (b, 0, 0))


class _At:
    def __init__(self, array, *lead):
        self.array, self.lead = array, lead


def _w_spec(w):
    shape = w.array.shape
    block = (None,) * len(w.lead) + shape[len(w.lead):]
    index = tuple(w.lead) + (0,) * (len(shape) - len(w.lead))
    return pl.BlockSpec(block, lambda *_: index, pipeline_mode=pl.Buffered(1))


def _ffn_apply(x, mod_ref, ffn_refs, j, per_row):
    if not per_row and x.shape[0] > FFN_CHAIN_ROWS:
        parts = [_ffn_apply(x[r:r + FFN_CHAIN_ROWS], mod_ref, ffn_refs, j, per_row)
                 for r in range(0, x.shape[0], FFN_CHAIN_ROWS)]
        return jnp.concatenate(parts, axis=0)
    gpre_ref, gpost_ref, wg_ref, wu_ref, wd_ref = ffn_refs
    h = _flat(_pre(x, mod_ref, gpre_ref[...], j, per_row)).astype(BF16)
    g = _dot(h, wg_ref[...])
    u = _dot(h, wu_ref[...])
    a = ((g * jax.nn.sigmoid(g)) * u).astype(BF16)
    y = _dot(a, wd_ref[...]).reshape(x.shape)
    return _post(x, y, mod_ref, gpost_ref[...], j, FFN_HALF, per_row)


def _ffn_kernel(x_ref, mod_ref, *refs, j, per_row):
    refs[-1][...] = _ffn_apply(x_ref[...], mod_ref, refs[:-1], j, per_row)


def _ffn(x, mod, layer, j, gpre, gpost, wg, wu, wd):
    r = _Rows(x, mod, layer, FFN_ROW_TILE)
    d = x.shape[-1]
    return pl.pallas_call(
        functools.partial(_ffn_kernel, j=j, per_row=r.per_row),
        grid=r.grid,
        in_specs=[r.x_spec, r.mod_spec, _w_spec(gpre), _w_spec(gpost), _w_spec(wg), _w_spec(wu), _w_spec(wd)],
        out_specs=r.x_spec,
        out_shape=jax.ShapeDtypeStruct(x.shape, F32),
        compiler_params=_params("arbitrary", "arbitrary"),
        name="ffn",
    )(x, mod, gpre.array, gpost.array, wg.array, wu.array, wd.array)


def _qkv_kernel(x_ref, mod_ref, gpre_ref, w_ref, q_ref, k_ref, v_ref, *bf16_refs, per_row, q_scale):
    x = x_ref[...]
    d = x.shape[-1]
    h = _flat(_pre(x, mod_ref, gpre_ref[...], 1, per_row)).astype(BF16)
    qkv = _dot(h, w_ref[...])
    k = qkv[:, d:2 * d]
    v = qkv[:, 2 * d:]
    q_ref[...] = (qkv[:, :d] * q_scale).reshape(q_ref.shape).astype(q_ref.dtype)
    if bf16_refs:
        k_ref[...] = k.T
        v_ref[...] = v.T
        bf16_refs[0][...] = k.astype(BF16)
        bf16_refs[1][...] = v.astype(BF16)
    else:
        k_ref[...] = k.reshape(k_ref.shape)
        v_ref[...] = v.reshape(v_ref.shape)


def _qkv_sample(x, mod, layer, gpre, w):
    r = _Rows(x, mod, layer)
    d = x.shape[-1]
    return pl.pallas_call(
        functools.partial(_qkv_kernel, per_row=True, q_scale=(d // N_HEADS) ** -0.5),
        grid=r.grid,
        in_specs=[r.x_spec, r.mod_spec, _w_spec(gpre), _w_spec(w)],
        out_specs=[r.x_spec] * 3,
        out_shape=[jax.ShapeDtypeStruct(x.shape, F32)] * 3,
        compiler_params=_params("arbitrary", "arbitrary"),
        name="qkv_proj_sample",
    )(x, mod, gpre.array, w.array)


def _qkv_prompt(x, mod, layer, gpre, w):
    r = _Rows(x, mod, layer)
    bsz, t, d = x.shape
    t_spec = pl.BlockSpec((None, d, r.tm), lambda b, i: (b, 0, i))
    t_shape = jax.ShapeDtypeStruct((bsz, d, t), F32)
    q, kt, vt, kb, vb = pl.pallas_call(
        functools.partial(_qkv_kernel, per_row=False, q_scale=(d // N_HEADS) ** -0.5),
        grid=r.grid,
        in_specs=[r.x_spec, r.mod_spec, _w_spec(gpre), _w_spec(w)],
        out_specs=[r.x_spec, t_spec, t_spec, r.x_spec, r.x_spec],
        out_shape=[jax.ShapeDtypeStruct(x.shape, BF16), t_shape, t_shape,
                   jax.ShapeDtypeStruct(x.shape, BF16), jax.ShapeDtypeStruct(x.shape, BF16)],
        compiler_params=_params("arbitrary", "arbitrary"),
        name="qkv_proj_prompt",
    )(x, mod, gpre.array, w.array)
    return q, kb, vb, (kt, vt)


def _out_proj_kernel(x_ref, o_ref, mod_ref, gpost_ref, w_ref, *refs, per_row):
    def chain(rows):
        x = x_ref[rows]
        y = _dot(_flat(o_ref[rows]).astype(BF16), w_ref[...]).reshape(x.shape)
        x = _post(x, y, mod_ref, gpost_ref[...], 1, 1.0, per_row)
        refs[-1][rows] = _ffn_apply(x, mod_ref, refs[:-1], 2, per_row)

    if per_row:
        chain(slice(None))
    else:
        for r in range(0, x_ref.shape[0], FFN_CHAIN_ROWS):
            chain(slice(r, r + FFN_CHAIN_ROWS))


def _out_proj(x, o, mod, layer, gpost, w, *ffn_params):
    r = _Rows(x, mod, layer, FFN_ROW_TILE)
    return pl.pallas_call(
        functools.partial(_out_proj_kernel, per_row=r.per_row),
        grid=r.grid,
        in_specs=[r.x_spec, r.x_spec, r.mod_spec, _w_spec(gpost), _w_spec(w)] + [_w_spec(p) for p in ffn_params],
        out_specs=r.x_spec,
        out_shape=jax.ShapeDtypeStruct(x.shape, F32),
        compiler_params=_params("arbitrary", "arbitrary"),
        name="out_proj_ffn",
    )(x, o, mod, gpost.array, w.array, *(p.array for p in ffn_params))


def _suffix_ones(n):
    j = lax.broadcasted_iota(jnp.int32, (n, n), 0)
    s = lax.broadcasted_iota(jnp.int32, (n, n), 1)
    return (j >= s).astype(BF16)


def _dot_nt(a, b):
    return lax.dot_general(a, b, (((1,), (1,)), ((), ())), preferred_element_type=F32)


def _sb_logits(z, uu, mask):
    sp = _softplus(z)
    if mask is not None:
        sp = jnp.where(mask, sp, 0.0)
        z = jnp.where(mask, z, MASKED_LOGIT)
    sub = uu.shape[1]
    parts, totals = [], []
    for s in range(z.shape[1] // sub):
        cols = slice(s * sub, (s + 1) * sub)
        suffix = _dot(sp[:, cols].astype(BF16), uu)
        parts.append(z[:, cols] - suffix)
        totals.append(jnp.broadcast_to(suffix[:, 0:1], (z.shape[0], LANES)))
    return (parts[0] if len(parts) == 1 else jnp.concatenate(parts, axis=1)), totals


def _sb_weights(t, totals, run_ref):
    run = run_ref[...]
    sub = t.shape[1] // len(totals)
    args = [None] * len(totals)
    for s in reversed(range(len(totals))):
        args[s] = t[:, s * sub:(s + 1) * sub] - jnp.concatenate([run] * (sub // LANES), axis=1)
        run = run + totals[s]
    run_ref[...] = run
    return jnp.exp(args[0] if len(args) == 1 else jnp.concatenate(args, axis=1)).astype(BF16)


def _attn_prompt_kernel(bias_ref, q_ref, k_ref, v_ref, uu_ref, *refs, head_dim, n_stack):
    tiles = refs[:2 * n_stack]
    o_ref, *stacked_refs = refs[2 * n_stack:len(refs) - 2]
    run_ref, acc_ref = refs[-2:]
    for s in range(n_stack):
        stacked_refs[0][s] = tiles[s][...]
        stacked_refs[1][s] = tiles[n_stack + s][...]
    group = pl.program_id(1)
    i = pl.program_id(2)
    tq = q_ref.shape[0]
    pair = 2 * head_dim
    n_pairs = q_ref.shape[1] // pair
    lane = lax.broadcasted_iota(jnp.int32, (tq, pair), 1)
    row = lax.broadcasted_iota(jnp.int32, (2 * tq, tq), 0)
    col = lax.broadcasted_iota(jnp.int32, (2 * tq, tq), 1)
    mask = col < jnp.where(row < tq, row, row - tq)
    uu = uu_ref[...]
    run_ref[...] = jnp.zeros_like(run_ref)
    acc_ref[...] = jnp.zeros_like(acc_ref)

    def keys(j):
        return pl.ds(pl.multiple_of(j * tq, tq), tq)

    def stacked_q(p):
        q = q_ref[:, p * pair:(p + 1) * pair]
        zero = jnp.zeros_like(q)
        return jnp.concatenate([jnp.where(lane < head_dim, q, zero), jnp.where(lane >= head_dim, q, zero)], axis=0)

    qqs = [stacked_q(p) for p in range(n_pairs)]

    def block(j, mask):
        for p in range(n_pairs):
            lanes = slice(p * pair, (p + 1) * pair)
            z = _dot_nt(qqs[p], k_ref[keys(j), lanes])
            head = 2 * (group * n_pairs + p)
            z = jnp.concatenate([z[:tq] + bias_ref[head], z[tq:] + bias_ref[head + 1]], axis=0)
            a = _sb_weights(*_sb_logits(z, uu, mask), run_ref.at[p])
            acc_ref[p] += _dot(a, v_ref[keys(j), lanes])

    block(i, mask)

    def body(n, carry):
        block(i - 1 - n, None)
        return carry

    lax.fori_loop(0, i, body, 0)
    for p in range(n_pairs):
        acc = acc_ref[p]
        o_ref[:, p * pair:(p + 1) * pair] = jnp.where(lane < head_dim, acc[:tq], acc[tq:]).astype(o_ref.dtype)


def _attn_prompt(q, kb, vb, bias, stack=()):
    bsz, t, d = q.shape
    tq = min(ATTN_TILE, t)
    sub = min(ATTN_SUB, tq)
    head_dim = d // N_HEADS
    pair = 2 * head_dim
    width = ATTN_PAIRS * pair
    groups = d // width
    n_stack = len(stack)
    blk = pl.BlockSpec((None, tq, width), lambda b, g, i: (b, i, g))
    seq = pl.BlockSpec((None, t, width), lambda b, g, i: (b, 0, g))
    tile = pl.BlockSpec((None, d // groups, tq), lambda b, g, i: (b, g, i))
    stacked = pl.BlockSpec((n_stack, None, d // groups, tq), lambda b, g, i: (0, b, g, i))
    out = pl.pallas_call(
        functools.partial(_attn_prompt_kernel, head_dim=head_dim, n_stack=n_stack),
        grid=(bsz, groups, t // tq),
        in_specs=[pl.BlockSpec(memory_space=pltpu.SMEM), blk, seq, seq,
                  pl.BlockSpec((sub, sub), lambda b, g, i: (0, 0))] + [tile] * (2 * n_stack),
        out_specs=[blk] + [stacked] * (2 if n_stack else 0),
        out_shape=[jax.ShapeDtypeStruct(q.shape, BF16)]
        + [jax.ShapeDtypeStruct((n_stack, bsz, d, t), F32)] * (2 if n_stack else 0),
        scratch_shapes=[pltpu.VMEM((ATTN_PAIRS, 2 * tq, LANES), F32), pltpu.VMEM((ATTN_PAIRS, 2 * tq, pair), F32)],
        compiler_params=_params("arbitrary", "arbitrary", "arbitrary"),
        name="sb_attn_prompt",
    )(bias, q, kb, vb, _suffix_ones(sub), *(p[0] for p in stack), *(p[1] for p in stack))
    return out if n_stack else out[0]


def _attn_sample_kernel(pt_ref, brow_ref, q_ref, k_ref, v_ref, *rest, head_dim, pages_per_step):
    pg = pages_per_step
    kp_refs, vp_refs = rest[:pg], rest[pg:2 * pg]
    uu_ref, o_ref, qbd_ref, knew_ref, vnew_ref, run_ref, acc_ref = rest[2 * pg:]
    b = pl.program_id(0)
    p = pl.program_id(1)
    s, _, d = q_ref.shape
    rows = s * N_HEADS
    row = lax.broadcasted_iota(jnp.int32, (rows, d), 0)
    lane = lax.broadcasted_iota(jnp.int32, (rows, d), 1)
    own_head = (lane // head_dim) == (row % N_HEADS)
    bias = brow_ref[...]
    uu = uu_ref[...]

    @pl.when(p == 0)
    def _():
        rep = jnp.concatenate([jnp.broadcast_to(q_ref[t, pl.ds(b, 1), :], (N_HEADS, d)) for t in range(s)], axis=0)
        qbd_ref[...] = jnp.where(own_head, rep, 0.0).astype(BF16)
        run_ref[...] = jnp.zeros_like(run_ref)
        acc_ref[...] = jnp.zeros_like(acc_ref)
        pad = 2 * SUBLANES
        r16 = lax.broadcasted_iota(jnp.int32, (pad, d), 0)
        kn = jnp.zeros((pad, d), F32)
        vn = jnp.zeros((pad, d), F32)
        for t in range(s):
            kn = jnp.where(r16 == t, jnp.broadcast_to(k_ref[t, pl.ds(b, 1), :], (pad, d)), kn)
            vn = jnp.where(r16 == t, jnp.broadcast_to(v_ref[t, pl.ds(b, 1), :], (pad, d)), vn)
        knew_ref[...] = jnp.zeros_like(knew_ref)
        vnew_ref[...] = jnp.zeros_like(vnew_ref)
        knew_ref[0:pad, :] = kn.astype(BF16)
        vnew_ref[0:pad, :] = vn.astype(BF16)
        r2 = lax.broadcasted_iota(jnp.int32, (rows, PAGE_SIZE), 0)
        c2 = lax.broadcasted_iota(jnp.int32, (rows, PAGE_SIZE), 1)
        new = _sb_logits(_dot_nt(qbd_ref[...], knew_ref[...]) + bias, uu, c2 < r2 // N_HEADS)
        acc_ref[...] += _dot(_sb_weights(*new, run_ref), vnew_ref[...])

    qbd = qbd_ref[...]
    z = jnp.concatenate([_dot(qbd, kp_refs[pg - 1 - c][...].astype(BF16)) + bias for c in range(pg)], axis=1)
    a = _sb_weights(*_sb_logits(z, uu, None), run_ref)
    acc = acc_ref[...]
    for c in range(pg):
        acc = acc + _dot_nt(a[:, c * PAGE_SIZE:(c + 1) * PAGE_SIZE], vp_refs[pg - 1 - c][...].astype(BF16))
    acc_ref[...] = acc

    @pl.when(p == pl.num_programs(1) - 1)
    def _():
        own = jnp.where(own_head, acc_ref[...], 0.0)
        o_ref[...] = jnp.sum(own.reshape(s, N_HEADS, d), axis=1)


def _attn_sample(q, k_new, v_new, cache_k, cache_v, page_table, bias, slot):
    s, db, d = q.shape
    n_pages = page_table.shape[1]
    head_dim = d // N_HEADS
    rows = s * N_HEADS
    pg = max(g for g in range(1, SAMPLE_PAGES_PER_STEP + 1) if n_pages % g == 0)
    bias_rows = jnp.broadcast_to(jnp.tile(bias, s)[:, None], (rows, PAGE_SIZE)).astype(F32)

    def page_spec(g):
        def page_map(b, p, pt):
            return (slot, pt[b * n_pages + n_pages - 1 - (p * pg + g)], 0, 0)
        return pl.BlockSpec((None, None, d, PAGE_SIZE), page_map)

    const = lambda shape: pl.BlockSpec(shape, lambda b, p, pt: (0,) * len(shape), pipeline_mode=pl.Buffered(1))
    grid_spec = pltpu.PrefetchScalarGridSpec(
        num_scalar_prefetch=1,
        grid=(db, n_pages // pg),
        in_specs=[const((rows, PAGE_SIZE)), const(q.shape), const(q.shape), const(q.shape)]
        + [page_spec(g) for g in range(pg)] * 2
        + [const((PAGE_SIZE, PAGE_SIZE))],
        out_specs=pl.BlockSpec((None, s, d), lambda b, p, pt: (b, 0, 0)),
        scratch_shapes=[
            pltpu.VMEM((rows, d), BF16),
            pltpu.VMEM((PAGE_SIZE, d), BF16),
            pltpu.VMEM((PAGE_SIZE, d), BF16),
            pltpu.VMEM((rows, PAGE_SIZE), F32),
            pltpu.VMEM((rows, d), F32),
        ],
    )
    return pl.pallas_call(
        functools.partial(_attn_sample_kernel, head_dim=head_dim, pages_per_step=pg),
        grid_spec=grid_spec,
        out_shape=jax.ShapeDtypeStruct((db, s, d), F32),
        compiler_params=_params("arbitrary", "arbitrary"),
        name="sb_attn_sample",
    )(page_table.reshape(-1), bias_rows, q, k_new, v_new, *([cache_k] * pg), *([cache_v] * pg),
      _suffix_ones(PAGE_SIZE))


def _shift_rows(u, prev, s):
    if s == SUBLANES:
        return jnp.concatenate([prev, u[:-SUBLANES]], axis=0)
    ru = pltpu.roll(u, s, axis=0)
    rp = pltpu.roll(prev, s, axis=0)
    r8 = lax.broadcasted_iota(jnp.int32, prev.shape, 0)
    return jnp.concatenate([jnp.where(r8 < s, rp, ru[:SUBLANES]), ru[SUBLANES:]], axis=0)


def _conv_prompt_kernel(x_ref, mod_ref, gpre_ref, gpost_ref, wbcx_ref, cw_ref, wout_ref, y_ref, tail_ref, carry_ref):
    @pl.when(pl.program_id(1) == 0)
    def _():
        carry_ref[...] = jnp.zeros_like(carry_ref)

    x = x_ref[...]
    d = x.shape[-1]
    h = _pre(x, mod_ref, gpre_ref[...], 1, False).astype(BF16)
    bcx = _dot(h, wbcx_ref[...])
    u = bcx[:, d:2 * d] * bcx[:, 2 * d:]
    prev = carry_ref[...]
    y = cw_ref[0:1, :] * _shift_rows(u, prev, 2) + cw_ref[1:2, :] * _shift_rows(u, prev, 1) + cw_ref[2:3, :] * u
    last = u[u.shape[0] - SUBLANES:, :]
    carry_ref[...] = last
    tail_ref[...] = last
    o = _dot((bcx[:, :d] * y).astype(BF16), wout_ref[...])
    y_ref[...] = _post(x, o, mod_ref, gpost_ref[...], 1, 1.0, False)


def _conv_prompt(x, mod, layer, gpre, gpost, wbcx, cw, wout):
    r = _Rows(x, mod, layer)
    bsz, _, d = x.shape
    return pl.pallas_call(
        _conv_prompt_kernel,
        grid=r.grid,
        in_specs=[r.x_spec, r.mod_spec, _w_spec(gpre), _w_spec(gpost), _w_spec(wbcx), _w_spec(cw), _w_spec(wout)],
        out_specs=[r.x_spec, r.tail_spec(SUBLANES, d)],
        out_shape=[jax.ShapeDtypeStruct(x.shape, F32), jax.ShapeDtypeStruct((bsz, SUBLANES, d), F32)],
        scratch_shapes=[pltpu.VMEM((SUBLANES, d), F32)],
        compiler_params=_params("arbitrary", "arbitrary"),
        name="conv_prompt",
    )(x, mod, gpre.array, gpost.array, wbcx.array, cw.array, wout.array)


def _conv_sample_kernel(x_ref, st_ref, mod_ref, gpre_ref, gpost_ref, wbcx_ref, cw_ref, wout_ref, y_ref, ns_ref):
    x = x_ref[...]
    s, db, d = x.shape
    h = _flat(_pre(x, mod_ref, gpre_ref[...], 1, True)).astype(BF16)
    bcx = _dot(h, wbcx_ref[...])
    u = (bcx[:, d:2 * d] * bcx[:, 2 * d:]).reshape(s, db, d)
    ext = [st_ref[j] for j in range(CONV_W - 1)] + [u[t] for t in range(s)]
    y = jnp.stack([sum(cw_ref[j:j + 1, :] * ext[t + j] for j in range(CONV_W)) for t in range(s)], axis=0)
    for j in range(CONV_W - 1):
        ns_ref[j] = ext[s + j]
    o = _dot((bcx[:, :d] * _flat(y)).astype(BF16), wout_ref[...]).reshape(x.shape)
    y_ref[...] = _post(x, o, mod_ref, gpost_ref[...], 1, 1.0, True)


def _conv_sample(x, state, mod, layer, gpre, gpost, wbcx, cw, wout):
    r = _Rows(x, mod, layer)
    st_spec = pl.BlockSpec(state.shape, lambda b, t: (0, 0, 0))
    return pl.pallas_call(
        _conv_sample_kernel,
        grid=r.grid,
        in_specs=[r.x_spec, st_spec, r.mod_spec, _w_spec(gpre), _w_spec(gpost), _w_spec(wbcx), _w_spec(cw), _w_spec(wout)],
        out_specs=[r.x_spec, st_spec],
        out_shape=[jax.ShapeDtypeStruct(x.shape, F32), jax.ShapeDtypeStruct(state.shape, F32)],
        compiler_params=_params("arbitrary", "arbitrary"),
        name="conv_sample",
    )(x, state, mod, gpre.array, gpost.array, wbcx.array, cw.array, wout.array)


def _pool_mix(diff, wp_ref, scale):
    g = diff.shape[-1] // len(POOL_WINDOWS)
    parts = [_dot(diff[:, i * g:(i + 1) * g].astype(BF16), wp_ref[i]) for i in range(len(POOL_WINDOWS))]
    return jnp.concatenate(parts, axis=-1) * scale


def _pool_prompt_kernel(x_ref, mod_ref, gpre_ref, gpost_ref, wp_ref, ps_ref, *refs):
    ffn_refs, (y_ref, tail_ref, carry_ref) = refs[:-3], refs[-3:]
    it = pl.program_id(1)

    @pl.when(it == 0)
    def _():
        carry_ref[...] = jnp.zeros_like(carry_ref)

    x = x_ref[...]
    tm, d = x.shape
    g = d // len(POOL_WINDOWS)
    h = _pre(x, mod_ref, gpre_ref[...], 1, False)
    sums = [h]
    for k in range(len(POOL_WINDOWS)):
        cur = sums[-1]
        sums.append(cur + _shift_rows(cur, carry_ref[k], 2 ** k))
        carry_ref[k] = cur[tm - SUBLANES:, :]
    assert POOL_WINDOWS == tuple(2 ** (k + 1) for k in range(len(POOL_WINDOWS)))
    total = jnp.concatenate([sums[k + 1][:, k * g:(k + 1) * g] for k in range(len(POOL_WINDOWS))], axis=-1)
    lane = lax.broadcasted_iota(jnp.int32, (tm, d), 1)
    pos = lax.broadcasted_iota(jnp.int32, (tm, d), 0) + it * tm
    window = jnp.left_shift(2, lane // g)
    count = jnp.minimum(pos + 1, window).astype(F32)
    diff = total / count - h
    tail_ref[...] = h[tm - 2 * SUBLANES:, :]
    x = _post(x, _pool_mix(diff, wp_ref, ps_ref[...]), mod_ref, gpost_ref[...], 1, 1.0, False)
    y_ref[...] = _ffn_apply(x, mod_ref, ffn_refs, 2, False)


def _pool_prompt(x, mod, layer, gpre, gpost, wp, ps, *ffn_params):
    r = _Rows(x, mod, layer)
    bsz, _, d = x.shape
    return pl.pallas_call(
        _pool_prompt_kernel,
        grid=r.grid,
        in_specs=[r.x_spec, r.mod_spec] + [_w_spec(p) for p in (gpre, gpost, wp, ps) + ffn_params],
        out_specs=[r.x_spec, r.tail_spec(2 * SUBLANES, d)],
        out_shape=[jax.ShapeDtypeStruct(x.shape, F32), jax.ShapeDtypeStruct((bsz, 2 * SUBLANES, d), F32)],
        scratch_shapes=[pltpu.VMEM((len(POOL_WINDOWS), SUBLANES, d), F32)],
        compiler_params=_params("arbitrary", "arbitrary"),
        name="pool_ffn_prompt",
    )(x, mod, *(p.array for p in (gpre, gpost, wp, ps) + ffn_params))


def _pool_sample_kernel(x_ref, st_ref, mod_ref, gpre_ref, gpost_ref, wp_ref, ps_ref, *refs, past_len):
    ffn_refs, (y_ref, ns_ref) = refs[:-2], refs[-2:]
    x = x_ref[...]
    s, db, d = x.shape
    g = d // len(POOL_WINDOWS)
    hist = st_ref.shape[0]
    h = _pre(x, mod_ref, gpre_ref[...], 1, True)
    ext = [st_ref[j] for j in range(hist)] + [h[t] for t in range(s)]
    means = []
    for t in range(s):
        parts = []
        for k, w in enumerate(POOL_WINDOWS):
            sl = slice(k * g, (k + 1) * g)
            tot = ext[hist + t][:, sl]
            for j in range(1, w):
                tot = tot + ext[hist + t - j][:, sl]
            parts.append(tot / float(min(past_len + t + 1, w)))
        means.append(jnp.concatenate(parts, axis=-1))
    diff = _flat(jnp.stack(means, axis=0) - h)
    for j in range(hist):
        ns_ref[j] = ext[s + j]
    mixed = _pool_mix(diff, wp_ref, ps_ref[...]).reshape(x.shape)
    x = _post(x, mixed, mod_ref, gpost_ref[...], 1, 1.0, True)
    y_ref[...] = _ffn_apply(x, mod_ref, ffn_refs, 2, True)


def _pool_sample(x, state, mod, layer, past_len, gpre, gpost, wp, ps, *ffn_params):
    r = _Rows(x, mod, layer)
    st_spec = pl.BlockSpec(state.shape, lambda b, t: (0, 0, 0))
    return pl.pallas_call(
        functools.partial(_pool_sample_kernel, past_len=past_len),
        grid=r.grid,
        in_specs=[r.x_spec, st_spec, r.mod_spec] + [_w_spec(p) for p in (gpre, gpost, wp, ps) + ffn_params],
        out_specs=[r.x_spec, st_spec],
        out_shape=[jax.ShapeDtypeStruct(x.shape, F32), jax.ShapeDtypeStruct(state.shape, F32)],
        compiler_params=_params("arbitrary", "arbitrary"),
        name="pool_ffn_sample",
    )(x, state, mod, *(p.array for p in (gpre, gpost, wp, ps) + ffn_params))


def kernel(x_prompt, x_sample, cache_k, cache_v, state_conv, state_pool, page_table, c_prompt, c_sample,
           w_ada, b_ada, g_pre, g_post, w_ffn_gate, w_ffn_up, w_ffn_down, w_qkv, w_o, sb_bias, w_bcx,
           conv_w, w_conv_out, w_pool, pool_scale):
    bsz, seq, d = x_prompt.shape
    db, dec_seq, _ = x_sample.shape
    depth = w_ada.shape[0]
    head_dim = d // N_HEADS
    n_pages = page_table.shape[1]
    past_len = n_pages * PAGE_SIZE
    assert dec_seq <= 2 * SUBLANES and db % SUBLANES == 0

    mod_all = _ada(jnp.concatenate([c_sample, c_prompt], axis=0), w_ada, b_ada)
    mod_p = jnp.swapaxes(mod_all[:, :, db:], 1, 2)

    wg, wu, wd = (w.astype(BF16) for w in (w_ffn_gate, w_ffn_up, w_ffn_down))
    wqkv, wo, wbcx, wco, wpl = (w.astype(BF16) for w in (w_qkv, w_o, w_bcx, w_conv_out, w_pool))
    pages = lambda c: jnp.transpose(c, (0, 1, 3, 4, 2)).reshape(c.shape[0], c.shape[1], d, PAGE_SIZE)
    cache_k4, cache_v4 = pages(cache_k), pages(cache_v)

    xp = x_prompt
    xs = jnp.swapaxes(x_sample, 0, 1)
    st_conv = jnp.swapaxes(state_conv, 1, 2)
    st_pool = jnp.swapaxes(state_pool, 1, 2)

    conv_p, pool_p, k_s, v_s, conv_s, pool_s = ([] for _ in range(6))
    kv_t = []
    g_pre4 = g_pre.reshape(depth, N_SUB, 1, d)
    g_post4 = g_post.reshape(depth, N_SUB, 1, d)
    pool_scale3 = pool_scale.reshape(pool_scale.shape[0], 1, d)
    for i in range(depth):
        gp = lambda j: _At(g_pre4, i, j)
        gq = lambda j: _At(g_post4, i, j)
        kind, slot = i % N_MIXERS, i // N_MIXERS
        ffn_w = lambda j: (_At(wg, i, j), _At(wu, i, j), _At(wd, i, j))

        xp = _ffn(xp, mod_p, i, 0, gp(0), gq(0), *ffn_w(0))
        xs = _ffn(xs, mod_all, i, 0, gp(0), gq(0), *ffn_w(0))

        ffn2 = (gp(2), gq(2), *ffn_w(1))
        if kind == 0:
            q, kb, vb, kv = _qkv_prompt(xp, mod_p, i, gp(1), _At(wqkv, slot))
            kv_t.append(kv)
            if slot == w_qkv.shape[0] - 1:
                o, *kv_stacked = _attn_prompt(q, kb, vb, sb_bias[slot], kv_t)
            else:
                o = _attn_prompt(q, kb, vb, sb_bias[slot])
            xp = _out_proj(xp, o, mod_p, i, gq(1), _At(wo, slot), *ffn2)

            q, k, v = _qkv_sample(xs, mod_all, i, gp(1), _At(wqkv, slot))
            to_seq = lambda a: jnp.swapaxes(a, 0, 1)
            o = _attn_sample(q, k, v, cache_k4, cache_v4, page_table, sb_bias[slot], slot)
            xs = _out_proj(xs, to_seq(o), mod_all, i, gq(1), _At(wo, slot), *ffn2)
            k_s.append(to_seq(k).reshape(db, dec_seq, N_HEADS, head_dim))
            v_s.append(to_seq(v).reshape(db, dec_seq, N_HEADS, head_dim))
        elif kind == 1:
            conv_ws = (_At(wbcx, slot), _At(conv_w, slot), _At(wco, slot))
            xp, tail = _conv_prompt(xp, mod_p, i, gp(1), gq(1), *conv_ws)
            conv_p.append(tail[:, SUBLANES - (CONV_W - 1):])
            xs, ns = _conv_sample(xs, st_conv[slot], mod_all, i, gp(1), gq(1), *conv_ws)
            conv_s.append(jnp.swapaxes(ns, 0, 1))
            xp = _ffn(xp, mod_p, i, 2, *ffn2)
            xs = _ffn(xs, mod_all, i, 2, *ffn2)
        else:
            pool_ws = (_At(wpl, slot), _At(pool_scale3, slot))
            xp, tail = _pool_prompt(xp, mod_p, i, gp(1), gq(1), *pool_ws, *ffn2)
            pool_p.append(tail[:, 2 * SUBLANES - POOL_HIST:])
            xs, ns = _pool_sample(xs, st_pool[slot], mod_all, i, past_len, gp(1), gq(1), *pool_ws, *ffn2)
            pool_s.append(jnp.swapaxes(ns, 0, 1))

    heads = lambda a: jnp.transpose(a.reshape(a.shape[0], bsz, N_HEADS, head_dim, seq), (0, 1, 4, 2, 3))
    return (xp, jnp.swapaxes(xs, 0, 1), heads(kv_stacked[0]), heads(kv_stacked[1]), jnp.stack(conv_p), jnp.stack(pool_p),
            jnp.stack(k_s), jnp.stack(v_s), jnp.stack(conv_s), jnp.stack(pool_s))
```
